```python
import jax, jax.numpy as jnp
from jax import lax
import numpy as np

D_MODEL = 2048
BATCH = 1
SEQ = 8192
DEPTH = 4
DEC_BATCH = 32
DEC_SEQ = 16
PAST_LEN = 4096

CHUNK = 64
D_MIX = D_MODEL
D_A = D_MIX // 4
D_B = D_MIX // 4
D_C = D_MIX // 4
D_D = D_MIX // 4
H_C = 4
DH_C = D_C // H_C
H_D = 4
DK_D = D_D // H_D
DV_D = D_D // H_D
A_CONV = 3
B_CONV = 31
Q_BLOCK = 128
EPS = 1e-6
N_IN = 4 * D_A + 3 * D_B + 4 * D_C + H_C + 4 * D_D

kernel_name = 'hybrid_stream_encoder_step'


def _split_points():
    sizes = [D_A] * 4 + [D_B] * 3 + [D_C] * 4 + [H_C] + [D_D] * 4
    return [int(s) for s in np.cumsum(sizes)[:-1]]


def rmsnorm(x, w):
    xf = x.astype(jnp.float32)
    y = xf * lax.rsqrt(jnp.mean(xf * xf, axis=-1, keepdims=True) + EPS)
    return (y * w.astype(jnp.float32)).astype(x.dtype)


def layernorm(x, w, b):
    xf = x.astype(jnp.float32)
    mu = jnp.mean(xf, axis=-1, keepdims=True)
    var = jnp.mean(jnp.square(xf - mu), axis=-1, keepdims=True)
    y = (xf - mu) * lax.rsqrt(var + EPS)
    return (y * w.astype(jnp.float32) + b.astype(jnp.float32)).astype(x.dtype)


def causal_dwconv(xp, w):
    C = xp.shape[-1]
    return lax.conv_general_dilated(xp, w[:, None, :].astype(xp.dtype), window_strides=(1,), padding='VALID',
                                    dimension_numbers=('NWC', 'WIO', 'NWC'), feature_group_count=C)


def fox_block(q, dq, qpos, k, v, dk, kpos):
    s = jnp.einsum('bqhd,bkhd->bhqk', q, k, preferred_element_type=jnp.float32) * (DH_C ** -0.5)
    bias = jnp.transpose(dq, (0, 2, 1))[:, :, :, None] - jnp.transpose(dk, (0, 2, 1))[:, :, None, :]
    s = jnp.where(kpos[None, None, None, :] <= qpos[None, None, :, None], s + bias, -jnp.inf)
    p = jax.nn.softmax(s, axis=-1)
    return jnp.einsum('bhqk,bkhd->bqhd', p.astype(v.dtype), v)


def fox_prompt(q, k, v, logf):
    B, T, H, Dh = q.shape
    dcum = jnp.cumsum(logf, axis=1)
    pos = jnp.arange(T)
    nb = T // Q_BLOCK
    qb = q.reshape(B, nb, Q_BLOCK, H, Dh).transpose(1, 0, 2, 3, 4)
    db = dcum.reshape(B, nb, Q_BLOCK, H).transpose(1, 0, 2, 3)
    pb = pos.reshape(nb, Q_BLOCK)
    out = lax.map(lambda a: fox_block(a[0], a[1], a[2], k, v, dcum, pos), (qb, db, pb))
    return out.transpose(1, 0, 2, 3, 4).reshape(B, T, H, Dh)


def gla_chunked(q, k, v, logf, S0, chunk):
    B, T, H, DK = q.shape
    n = T // chunk

    def to_chunks(a):
        return a.astype(jnp.float32).reshape(B, n, chunk, H, a.shape[-1]).transpose(1, 0, 3, 2, 4)

    qc, kc, vc, gc = to_chunks(q), to_chunks(k), to_chunks(v), to_chunks(logf)
    mask = jnp.tril(jnp.ones((chunk, chunk), dtype=bool))

    def step(S, inp):
        qi, ki, vi, gi = inp
        b = jnp.cumsum(gi, axis=2)
        o_inter = jnp.einsum('bhtk,bhkv->bhtv', qi * jnp.exp(b), S)
        rel = jnp.where(mask[None, None, :, :, None], b[:, :, :, None, :] - b[:, :, None, :, :], -jnp.inf)
        A = jnp.einsum('bhtk,bhsk,bhtsk->bhts', qi, ki, jnp.exp(rel))
        o = o_inter + jnp.einsum('bhts,bhsv->bhtv', A, vi)
        bl = b[:, :, -1:, :]
        S_new = jnp.exp(bl[:, :, 0, :])[..., None] * S + jnp.einsum('bhsk,bhsv->bhkv', ki * jnp.exp(bl - b), vi)
        return S_new, o

    S, o = lax.scan(step, S0, (qc, kc, vc, gc))
    o = o.transpose(1, 0, 3, 2, 4).reshape(B, T, H, v.shape[-1])
    return o, S


def mixer_layer(x, c, p, lb, hist):
    B, T, _ = x.shape
    f32 = jnp.float32
    mod = jnp.einsum('bd,de->be', jax.nn.silu(c), p['w_ada']) + p['b_ada']
    shift, scale, gate = jnp.split(mod, 3, axis=-1)
    h = rmsnorm(x, p['norm']) * (1 + scale[:, None, :]) + shift[:, None, :]
    proj = jnp.einsum('btd,de->bte', h, p['w_in'])
    (a_b, a_c, a_x, a_z, b_a, b_g, b_z, c_q, c_k, c_v, c_z, c_f,
     d_q, d_f, d_i, d_z) = jnp.split(proj, _split_points(), axis=-1)

    a_hist = jnp.zeros((B, A_CONV - 1, D_A), proj.dtype) if hist is None else hist['a_conv'].astype(proj.dtype)
    a_seq = jnp.concatenate([a_hist, a_c * a_x], axis=1)
    y_a = jax.nn.silu(a_z) * a_b * causal_dwconv(a_seq, p['a_conv'])

    b_hist = jnp.zeros((B, B_CONV - 1, D_B), proj.dtype) if hist is None else hist['b_conv'].astype(proj.dtype)
    b_seq = jnp.concatenate([b_hist, b_a * jax.nn.sigmoid(b_g)], axis=1)
    b_y = causal_dwconv(b_seq, p['b_conv_w']) + p['b_conv_b']
    y_b = jax.nn.silu(b_z) * jax.nn.silu(layernorm(b_y, p['b_ln_w'], p['b_ln_b']))

    q = c_q.reshape(B, T, H_C, DH_C)
    k = c_k.reshape(B, T, H_C, DH_C)
    v = c_v.reshape(B, T, H_C, DH_C)
    logf_c = jax.nn.log_sigmoid((c_f + p['b_f']).astype(f32))
    if hist is None:
        o_c = fox_prompt(q, k, v, logf_c)
    else:
        P = hist['k'].shape[1]
        k_all = jnp.concatenate([hist['k'].astype(k.dtype), k], axis=1)
        v_all = jnp.concatenate([hist['v'].astype(v.dtype), v], axis=1)
        dcum = jnp.cumsum(jnp.concatenate([hist['logf'].astype(f32), logf_c], axis=1), axis=1)
        kpos = jnp.arange(P + T)
        qpos = P + jnp.arange(T)
        o_c = fox_block(q, dcum[:, P:], qpos, k_all, v_all, dcum, kpos)
    y_c = jax.nn.silu(c_z) * o_c.reshape(B, T, D_C).astype(proj.dtype)

    lbh = lb.reshape(H_D, DK_D)
    gl = d_f.reshape(B, T, H_D, DK_D).astype(f32)
    logf_d = jnp.log(lbh + (1 - lbh) * jax.nn.sigmoid(gl))
    k_d = (1 - lbh) * jax.nn.sigmoid(-gl)
    if hist is None:
        S0 = jnp.zeros((B, H_D, DK_D, DV_D), f32)
        chunk = CHUNK
    else:
        S0 = hist['S'].astype(f32)
        chunk = T
    o_d, S = gla_chunked(d_q.reshape(B, T, H_D, DK_D), k_d, d_i.reshape(B, T, H_D, DV_D), logf_d, S0, chunk)
    y_d = jax.nn.silu(d_z) * rmsnorm(o_d, p['d_norm']).reshape(B, T, D_D).astype(proj.dtype)

    y = jnp.concatenate([y_a, y_b, y_c, y_d], axis=-1)
    x = (x + gate[:, None, :] * jnp.einsum('bte,ed->btd', y, p['w_out'])).astype(x.dtype)
    new = (a_seq[:, -(A_CONV - 1):], b_seq[:, -(B_CONV - 1):], k, v, logf_c, S)
    return x, new


def setup_inputs(seed: int = 0) -> dict:
    key = jax.random.key(seed)
    ks = jax.random.split(key, 24)
    nrm = jax.random.normal
    f32 = jnp.float32
    return {
        'x_prompt': nrm(ks[0], (BATCH, SEQ, D_MODEL), f32),
        'x_sample': nrm(ks[1], (DEC_BATCH, DEC_SEQ, D_MODEL), f32),
        'c_prompt': nrm(ks[2], (BATCH, D_MODEL), f32),
        'c_sample': nrm(ks[3], (DEC_BATCH, D_MODEL), f32),
        'cache_a_conv': nrm(ks[4], (DEPTH, DEC_BATCH, A_CONV - 1, D_A), f32),
        'cache_b_conv': 0.5 * nrm(ks[5], (DEPTH, DEC_BATCH, B_CONV - 1, D_B), f32),
        'cache_k': nrm(ks[6], (DEPTH, DEC_BATCH, PAST_LEN, H_C, DH_C), f32),
        'cache_v': nrm(ks[7], (DEPTH, DEC_BATCH, PAST_LEN, H_C, DH_C), f32),
        'cache_logf': jax.nn.log_sigmoid(2.0 + nrm(ks[8], (DEPTH, DEC_BATCH, PAST_LEN, H_C), f32)),
        'state_hgrn': 0.5 * nrm(ks[9], (DEPTH, DEC_BATCH, H_D, DK_D, DV_D), f32),
        'norm_w': 1.0 + 0.02 * nrm(ks[10], (DEPTH, D_MODEL), f32),
        'w_ada': nrm(ks[11], (DEPTH, D_MODEL, 3 * D_MODEL), f32) * (0.2 * D_MODEL ** -0.5),
        'b_ada': 0.02 * nrm(ks[12], (DEPTH, 3 * D_MODEL), f32),
        'w_in': nrm(ks[13], (DEPTH, D_MODEL, N_IN), f32) * (D_MODEL ** -0.5),
        'b_f': 2.0 + 0.1 * nrm(ks[14], (DEPTH, H_C), f32),
        'a_conv_w': nrm(ks[15], (DEPTH, A_CONV, D_A), f32) * (A_CONV ** -0.5),
        'b_conv_w': nrm(ks[16], (DEPTH, B_CONV, D_B), f32) * (B_CONV ** -0.5),
        'b_conv_b': 0.02 * nrm(ks[17], (DEPTH, D_B), f32),
        'b_ln_w': 1.0 + 0.02 * nrm(ks[18], (DEPTH, D_B), f32),
        'b_ln_b': 0.02 * nrm(ks[19], (DEPTH, D_B), f32),
        'd_norm_w': 1.0 + 0.02 * nrm(ks[20], (DEPTH, DV_D), f32),
        'hgrn_lb_logits': 0.5 * nrm(ks[21], (DEPTH, D_D), f32),
        'w_out': nrm(ks[22], (DEPTH, D_MIX, D_MODEL), f32) * (D_MIX ** -0.5),
        'final_norm_w': 1.0 + 0.02 * nrm(ks[23], (D_MODEL,), f32),
    }


def reference(x_prompt, x_sample, c_prompt, c_sample, cache_a_conv, cache_b_conv, cache_k, cache_v,
              cache_logf, state_hgrn, norm_w, w_ada, b_ada, w_in, b_f, a_conv_w, b_conv_w, b_conv_b,
              b_ln_w, b_ln_b, d_norm_w, hgrn_lb_logits, w_out, final_norm_w):
    lbs = jnp.cumsum(jax.nn.softmax(hgrn_lb_logits.astype(jnp.float32), axis=0), axis=0)
    lbs = lbs - lbs[0:1]
    xp, xs = x_prompt, x_sample
    prm = [[] for _ in range(6)]
    smp = [[] for _ in range(6)]
    for l in range(DEPTH):
        p = {'norm': norm_w[l], 'w_ada': w_ada[l], 'b_ada': b_ada[l], 'w_in': w_in[l], 'b_f': b_f[l],
             'a_conv': a_conv_w[l], 'b_conv_w': b_conv_w[l], 'b_conv_b': b_conv_b[l], 'b_ln_w': b_ln_w[l],
             'b_ln_b': b_ln_b[l], 'd_norm': d_norm_w[l], 'w_out': w_out[l]}
        hist = {'a_conv': cache_a_conv[l], 'b_conv': cache_b_conv[l], 'k': cache_k[l], 'v': cache_v[l],
                'logf': cache_logf[l], 'S': state_hgrn[l]}
        xp, sp = mixer_layer(xp, c_prompt, p, lbs[l], None)
        xs, ss = mixer_layer(xs, c_sample, p, lbs[l], hist)
        for i in range(6):
            prm[i].append(sp[i])
            smp[i].append(ss[i])
    y_prompt = rmsnorm(xp, final_norm_w)
    y_sample = rmsnorm(xs, final_norm_w)
    new_a_conv_prompt = jnp.stack(prm[0])
    new_a_conv_sample = jnp.stack(smp[0])
    new_b_conv_prompt = jnp.stack(prm[1])
    new_b_conv_sample = jnp.stack(smp[1])
    new_k_prompt = jnp.stack(prm[2])
    new_k_sample = jnp.stack(smp[2])
    new_v_prompt = jnp.stack(prm[3])
    new_v_sample = jnp.stack(smp[3])
    new_logf_prompt = jnp.stack(prm[4])
    new_logf_sample = jnp.stack(smp[4])
    new_hgrn_prompt = jnp.stack(prm[5])
    new_hgrn_sample = jnp.stack(smp[5])
    return (y_prompt, y_sample, new_a_conv_prompt, new_a_conv_sample, new_b_conv_prompt, new_b_conv_sample,
            new_k_prompt, new_k_sample, new_v_prompt, new_v_sample, new_logf_prompt, new_logf_sample,
            new_hgrn_prompt, new_hgrn_sample)
```

```python
import functools

import jax
import jax.numpy as jnp
from jax import lax
from jax.experimental import pallas as pl
from jax.experimental.pallas import tpu as pltpu

F32 = jnp.float32
BF16 = jnp.bfloat16

DEPTH = 4
D_MODEL = 2048
D_GRP = 512
N_SPLIT = 15
H = 4
DH = 128
A_CONV = 3
B_CONV = 31
CHUNK = 64
EPS = 1e-6
LANES = 128
VMEM_LIMIT = 56 * 1024 * 1024

(S_AB, S_AC, S_AX, S_AZ, S_BA, S_BG, S_BZ, S_CQ, S_CK, S_CV, S_CZ, S_DQ, S_DF, S_DI, S_DZ) = range(15)

IN_TN = 1536
IN_TM = 512
CONV_TT = 256
ATT_T = 512
ATT_TK_CACHE = 1024
OUT_TM = 512


def _cparams(*sem):
    return pltpu.CompilerParams(dimension_semantics=sem, vmem_limit_bytes=VMEM_LIMIT)


def _sigmoid(x):
    return 1.0 / (1.0 + jnp.exp(-x))


def _silu(x):
    return x * _sigmoid(x)


def _log_sigmoid(x):
    return jnp.minimum(x, 0.0) - jnp.log(1.0 + jnp.exp(-jnp.abs(x)))


def _split3(x):
    hi = x.astype(BF16)
    r1 = x - hi.astype(F32)
    mid = r1.astype(BF16)
    lo = (r1 - mid.astype(F32)).astype(BF16)
    return hi, mid, lo


def _dot(a, b):
    return jnp.dot(a, b, preferred_element_type=F32)


def _dot_nt(a, b):
    return lax.dot_general(a, b, (((1,), (1,)), ((), ())), preferred_element_type=F32)


def _ada_kernel(c_ref, w_ref, b_ref, o_ref):
    c = c_ref[...]
    a = _silu(c).astype(BF16)
    o_ref[...] = _dot(a, w_ref[...].astype(BF16)) + b_ref[...]


def ada_mod(c_all, w_ada, b_ada):
    L, D, N = w_ada.shape
    R = c_all.shape[0]
    tn = 768
    return pl.pallas_call(
        _ada_kernel,
        grid=(L, N // tn),
        in_specs=[pl.BlockSpec((R, D), lambda l, j: (0, 0)),
                  pl.BlockSpec((None, D, tn), lambda l, j: (l, 0, j)),
                  pl.BlockSpec((None, 1, tn), lambda l, j: (l, 0, j))],
        out_specs=pl.BlockSpec((None, R, tn), lambda l, j: (l, 0, j)),
        out_shape=jax.ShapeDtypeStruct((L, R, N), F32),
        compiler_params=_cparams("parallel", "parallel"),
        name="ada_mod",
    )(c_all, w_ada, b_ada.reshape(L, 1, N))


def _inproj_kernel(x_ref, nw_ref, sc_ref, sh_ref, w_ref, wf_ref, bf_ref,
                   proj_ref, k_ref, v_ref, logf_ref, h_scr):
    j = pl.program_id(1)

    @pl.when(j == 0)
    def _():
        x = x_ref[...]
        ms = jnp.mean(x * x, axis=-1, keepdims=True)
        y = x * lax.rsqrt(ms + EPS) * nw_ref[...]
        h = y * (1.0 + sc_ref[...]) + sh_ref[...]
        hb = h.reshape(h_scr.shape).astype(BF16)
        h_scr[...] = hb
        cf = _dot(hb, wf_ref[...]) + bf_ref[...]
        logf_ref[...] = _log_sigmoid(cf)

    h = h_scr[...]
    for c in range(IN_TN // D_GRP):
        r = _dot(h, w_ref[:, c * D_GRP:(c + 1) * D_GRP])
        proj_ref[:, c * D_GRP:(c + 1) * D_GRP] = r.astype(BF16)
        for split, ref in ((S_CK, k_ref), (S_CV, v_ref)):
            if split % 3 == c:
                @pl.when(j == split // 3)
                def _(r=r, ref=ref):
                    ref[...] = r


def in_proj(x, norm_w, scale, shift, w_main, w_f, b_f, *, bb, tt):
    B, T, D = x.shape
    rows = bb * tt
    nb, nt = B // bb, T // tt
    R = B * T
    N = w_main.shape[1]

    def xmap(i, j):
        return (i // nt, i % nt, 0)

    def mmap(i, j):
        return (i // nt, 0, 0)

    return pl.pallas_call(
        _inproj_kernel,
        grid=(nb * nt, N // IN_TN),
        in_specs=[pl.BlockSpec((bb, tt, D), xmap),
                  pl.BlockSpec((1, 1, D), lambda i, j: (0, 0, 0)),
                  pl.BlockSpec((bb, 1, D), mmap),
                  pl.BlockSpec((bb, 1, D), mmap),
                  pl.BlockSpec((D, IN_TN), lambda i, j: (0, j)),
                  pl.BlockSpec((D, LANES), lambda i, j: (0, 0)),
                  pl.BlockSpec((1, LANES), lambda i, j: (0, 0))],
        out_specs=[pl.BlockSpec((rows, IN_TN), lambda i, j: (i, j)),
                   pl.BlockSpec((rows, D_GRP), lambda i, j: (i, 0)),
                   pl.BlockSpec((rows, D_GRP), lambda i, j: (i, 0)),
                   pl.BlockSpec((rows, LANES), lambda i, j: (i, 0))],
        out_shape=[jax.ShapeDtypeStruct((R, N), BF16),
                   jax.ShapeDtypeStruct((R, D_GRP), F32),
                   jax.ShapeDtypeStruct((R, D_GRP), F32),
                   jax.ShapeDtypeStruct((R, LANES), F32)],
        scratch_shapes=[pltpu.VMEM((rows, D), BF16)],
        compiler_params=_cparams("parallel", "arbitrary"),
        name="in_proj",
    )(x, norm_w.reshape(1, 1, D), scale, shift, w_main, w_f, b_f)


HA = 8
HB = 32

def _conv_kernel(ab_ref, ac_ref, ax_ref, az_ref, ba_ref, bg_ref, bz_ref, ha_ref, hb_ref,
                 wa_ref, wb_ref, bb_ref, lw_ref, lb_ref,
                 ya_ref, yb_ref, na_ref, nb_ref, seqa, seqb, *, tt):
    t = pl.program_id(1)
    na, nb = A_CONV - 1, B_CONV - 1

    @pl.when(t == 0)
    def _():
        seqa[HA - na:HA, :] = ha_ref[0]
        seqb[HB - nb:HB, :] = hb_ref[0]

    @pl.when(t > 0)
    def _():
        ta = seqa[HA + tt - na:HA + tt, :]
        seqa[HA - na:HA, :] = ta
        tb = seqb[HB + tt - nb:HB + tt, :]
        seqb[HB - nb:HB, :] = tb

    seqa[HA:HA + tt, :] = ac_ref[0].astype(F32) * ax_ref[0].astype(F32)
    seqb[HB:HB + tt, :] = ba_ref[0].astype(F32) * _sigmoid(bg_ref[0].astype(F32))

    rc = min(tt, 32)
    for r0 in range(0, tt, rc):
        acc = wa_ref[0:1, :] * seqa[HA - na + r0:HA - na + r0 + rc, :]
        for w in range(1, A_CONV):
            acc = acc + wa_ref[w:w + 1, :] * seqa[HA - na + r0 + w:HA - na + r0 + w + rc, :]
        ya = _silu(az_ref[0, r0:r0 + rc, :].astype(F32)) * ab_ref[0, r0:r0 + rc, :].astype(F32) * acc
        ya_ref[0, r0:r0 + rc, :] = ya.astype(BF16)

        acc = wb_ref[0:1, :] * seqb[HB - nb + r0:HB - nb + r0 + rc, :]
        for w in range(1, B_CONV):
            acc = acc + wb_ref[w:w + 1, :] * seqb[HB - nb + r0 + w:HB - nb + r0 + w + rc, :]
        y = acc + bb_ref[...]
        mu = jnp.mean(y, axis=-1, keepdims=True)
        yc = y - mu
        var = jnp.mean(yc * yc, axis=-1, keepdims=True)
        yn = yc * lax.rsqrt(var + EPS) * lw_ref[...] + lb_ref[...]
        yb = _silu(bz_ref[0, r0:r0 + rc, :].astype(F32)) * _silu(yn)
        yb_ref[0, r0:r0 + rc, :] = yb.astype(BF16)

    na_ref[0] = seqa[HA + tt - na:HA + tt, :]
    nb_ref[0] = seqb[HB + tt - nb:HB + tt, :]


def conv_ab(proj3, hist_a, hist_b, wa, wb, bias_b, ln_w, ln_b, *, tt):
    B, T, _ = proj3.shape
    nt = T // tt

    def col(g):
        return pl.BlockSpec((1, tt, D_GRP), lambda b, t, g=g: (b, t, g))

    def full(a):
        return pl.BlockSpec(a.shape, lambda b, t: (0,) * a.ndim)

    bias_b, ln_w, ln_b = (a.reshape(1, D_GRP) for a in (bias_b, ln_w, ln_b))
    return pl.pallas_call(
        functools.partial(_conv_kernel, tt=tt),
        grid=(B, nt),
        in_specs=[col(S_AB), col(S_AC), col(S_AX), col(S_AZ), col(S_BA), col(S_BG), col(S_BZ),
                  pl.BlockSpec((1, A_CONV - 1, D_GRP), lambda b, t: (b, 0, 0)),
                  pl.BlockSpec((1, B_CONV - 1, D_GRP), lambda b, t: (b, 0, 0)),
                  full(wa), full(wb), full(bias_b), full(ln_w), full(ln_b)],
        out_specs=[pl.BlockSpec((1, tt, D_GRP), lambda b, t: (b, t, 0)),
                   pl.BlockSpec((1, tt, D_GRP), lambda b, t: (b, t, 0)),
                   pl.BlockSpec((1, A_CONV - 1, D_GRP), lambda b, t: (b, 0, 0)),
                   pl.BlockSpec((1, B_CONV - 1, D_GRP), lambda b, t: (b, 0, 0))],
        out_shape=[jax.ShapeDtypeStruct((B, T, D_GRP), BF16),
                   jax.ShapeDtypeStruct((B, T, D_GRP), BF16),
                   jax.ShapeDtypeStruct((B, A_CONV - 1, D_GRP), F32),
                   jax.ShapeDtypeStruct((B, B_CONV - 1, D_GRP), F32)],
        scratch_shapes=[pltpu.VMEM((HA + tt, D_GRP), F32), pltpu.VMEM((HB + tt, D_GRP), F32)],
        compiler_params=_cparams("parallel", "arbitrary"),
        name="conv_ab",
    )(proj3, proj3, proj3, proj3, proj3, proj3, proj3, hist_a, hist_b, wa, wb, bias_b, ln_w, ln_b)


def _cumsum_kernel(x_ref, o_ref, carry, *, cb):
    j = pl.program_id(0)

    @pl.when(j == 0)
    def _():
        carry[...] = jnp.zeros_like(carry)

    hi, mid, lo = _split3(x_ref[...])
    r = lax.broadcasted_iota(jnp.int32, (cb, cb), 0)
    c = lax.broadcasted_iota(jnp.int32, (cb, cb), 1)
    u = (r <= c).astype(BF16)
    cs = _dot(hi, u) + _dot(mid, u) + _dot(lo, u) + carry[:, 0:1]
    o_ref[...] = cs
    carry[...] = jnp.broadcast_to(cs[:, cb - 1:cb], carry.shape)


def cumsum_lanes(x):
    R, L = x.shape
    cb = 256 if L % 256 == 0 else LANES
    return pl.pallas_call(
        functools.partial(_cumsum_kernel, cb=cb),
        grid=(L // cb,),
        in_specs=[pl.BlockSpec((R, cb), lambda j: (0, j))],
        out_specs=pl.BlockSpec((R, cb), lambda j: (0, j)),
        out_shape=jax.ShapeDtypeStruct((R, L), F32),
        scratch_shapes=[pltpu.VMEM((R, LANES), F32)],
        compiler_params=_cparams("arbitrary"),
        name="cumsum_lanes",
    )(x)


def _fox_prompt_kernel(q_ref, k_ref, v_ref, z_ref, dq_ref, dk_ref, y_ref, *, tile):
    qi = pl.program_id(1)
    scale = DH ** -0.5
    q = q_ref[...]
    dq = dq_ref[...]

    def step(ki, carry, masked):
        m, l, acc = carry
        off = pl.multiple_of(ki * tile, tile)
        k = k_ref[pl.ds(off, tile), :]
        v = v_ref[pl.ds(off, tile), :]
        s = _dot_nt(q, k) * scale + (dq - dk_ref[:, pl.ds(off, tile)])
        if masked:
            r = lax.broadcasted_iota(jnp.int32, (tile, tile), 0)
            c = lax.broadcasted_iota(jnp.int32, (tile, tile), 1)
            s = jnp.where(c <= r, s, -jnp.inf)
        m_new = jnp.maximum(m, jnp.max(s, axis=-1, keepdims=True))
        alpha = jnp.exp(m - m_new)
        p = jnp.exp(s - m_new)
        l = alpha * l + jnp.sum(p, axis=-1, keepdims=True)
        acc = alpha * acc + _dot(p.astype(BF16), v)
        return m_new, l, acc

    init = (jnp.full((tile, 1), -jnp.inf, F32), jnp.zeros((tile, 1), F32), jnp.zeros((tile, DH), F32))
    carry = lax.fori_loop(0, qi, lambda ki, cr: step(ki, cr, False), init)
    m, l, acc = step(qi, carry, True)
    y_ref[...] = (_silu(z_ref[...].astype(F32)) * (acc / l)).astype(BF16)


def fox_prompt(proj, dq_col, dk_row, *, tile):
    T = proj.shape[0]
    cpb = D_GRP // DH
    return pl.pallas_call(
        functools.partial(_fox_prompt_kernel, tile=tile),
        grid=(H, T // tile),
        in_specs=[pl.BlockSpec((tile, DH), lambda h, i: (i, S_CQ * cpb + h)),
                  pl.BlockSpec((T, DH), lambda h, i: (0, S_CK * cpb + h)),
                  pl.BlockSpec((T, DH), lambda h, i: (0, S_CV * cpb + h)),
                  pl.BlockSpec((tile, DH), lambda h, i: (i, S_CZ * cpb + h)),
                  pl.BlockSpec((None, tile, 1), lambda h, i: (h, i, 0)),
                  pl.BlockSpec((None, 1, T), lambda h, i: (h, 0, 0))],
        out_specs=pl.BlockSpec((tile, DH), lambda h, i: (i, h)),
        out_shape=jax.ShapeDtypeStruct((T, D_GRP), BF16),
        compiler_params=_cparams("parallel", "arbitrary"),
        name="fox_prompt",
    )(proj, proj, proj, proj, dq_col, dk_row)


def _fox_sample_kernel(q_ref, kn_ref, vn_ref, z_ref, kc_ref, vc_ref, dkh_ref, dq_ref, dkn_ref,
                       y_ref, m_scr, l_scr, acc_scr, *, tq):
    kt = pl.program_id(1)
    nk = pl.num_programs(1)
    scale = DH ** -0.5

    @pl.when(kt == 0)
    def _():
        m_scr[...] = jnp.full(m_scr.shape, -jnp.inf, F32)
        l_scr[...] = jnp.zeros(l_scr.shape, F32)
        acc_scr[...] = jnp.zeros(acc_scr.shape, F32)

    def update(h, s, v):
        m = m_scr[h]
        m_new = jnp.maximum(m, jnp.max(s, axis=-1, keepdims=True))
        alpha = jnp.exp(m - m_new)
        p = jnp.exp(s - m_new)
        l_scr[h] = alpha * l_scr[h] + jnp.sum(p, axis=-1, keepdims=True)
        acc_scr[h] = alpha * acc_scr[h] + _dot(p.astype(BF16), v)
        m_scr[h] = m_new

    for h in range(H):
        cs = slice(h * DH, (h + 1) * DH)
        q = q_ref[0, :, cs]
        dq = dq_ref[0, :, h:h + 1]
        k = kc_ref[0, :, cs].astype(BF16)
        v = vc_ref[0, :, cs].astype(BF16)
        s = _dot_nt(q, k) * scale + (dq - dkh_ref[0, h:h + 1, :])
        update(h, s, v)

    @pl.when(kt == nk - 1)
    def _():
        r = lax.broadcasted_iota(jnp.int32, (tq, tq), 0)
        c = lax.broadcasted_iota(jnp.int32, (tq, tq), 1)
        for h in range(H):
            cs = slice(h * DH, (h + 1) * DH)
            q = q_ref[0, :, cs]
            dq = dq_ref[0, :, h:h + 1]
            s = _dot_nt(q, kn_ref[0, :, cs]) * scale + (dq - dkn_ref[0, h:h + 1, :])
            s = jnp.where(c <= r, s, -jnp.inf)
            update(h, s, vn_ref[0, :, cs])
            o = acc_scr[h] / l_scr[h]
            y_ref[0, :, cs] = (_silu(z_ref[0, :, cs].astype(F32)) * o).astype(BF16)


def fox_sample(proj3, cache_k, cache_v, layer, dkh, dq_col, dkn_row, *, tk):
    B, tq, _ = proj3.shape
    P = cache_k.shape[2]

    def col(g):
        return pl.BlockSpec((1, tq, D_GRP), lambda b, t, g=g: (b, 0, g))

    cache_spec = pl.BlockSpec((None, 1, tk, D_GRP), lambda b, t: (layer, b, t, 0))
    return pl.pallas_call(
        functools.partial(_fox_sample_kernel, tq=tq),
        grid=(B, P // tk),
        in_specs=[col(S_CQ), col(S_CK), col(S_CV), col(S_CZ), cache_spec, cache_spec,
                  pl.BlockSpec((1, H, tk), lambda b, t: (b, 0, t)),
                  pl.BlockSpec((1, tq, H), lambda b, t: (b, 0, 0)),
                  pl.BlockSpec((1, H, tq), lambda b, t: (b, 0, 0))],
        out_specs=pl.BlockSpec((1, tq, D_GRP), lambda b, t: (b, 0, 0)),
        out_shape=jax.ShapeDtypeStruct((B, tq, D_GRP), BF16),
        scratch_shapes=[pltpu.VMEM((H, tq, 1), F32), pltpu.VMEM((H, tq, 1), F32),
                        pltpu.VMEM((H, tq, DH), F32)],
        compiler_params=_cparams("parallel", "arbitrary"),
        name="fox_sample",
    )(proj3, proj3, proj3, proj3, cache_k, cache_v, dkh, dq_col, dkn_row)


def _lower_bound(lbl_ref, layer):
    x = lbl_ref[...]
    e = jnp.exp(x - jnp.max(x, axis=0, keepdims=True))
    sm = e / jnp.sum(e, axis=0, keepdims=True)
    cum = sm[0:1, :]
    first = cum
    for i in range(1, layer + 1):
        cum = cum + sm[i:i + 1, :]
    return cum - first


def _gla_group(q, gl, vi, lb, st_list, b_scr, *, C):
    G = len(st_list)
    N = G * C
    sig = _sigmoid(gl)
    g = jnp.log(lb + (1.0 - lb) * sig)
    kk = (1.0 - lb) * _sigmoid(-gl)

    row = lax.broadcasted_iota(jnp.int32, (N, N), 0)
    colm = lax.broadcasted_iota(jnp.int32, (N, N), 1)

    def same_block(size):
        sh = size.bit_length() - 1
        return lax.shift_right_logical(row, sh) == lax.shift_right_logical(colm, sh)

    tril = (same_block(C) & (colm <= row)).astype(BF16)
    g_hi, g_mid, g_lo = _split3(g)
    b = _dot(tril, g_hi) + _dot(tril, g_mid) + _dot(tril, g_lo)
    b_scr[...] = b

    def gather_rows(size, offset):
        parts = [jnp.broadcast_to(b_scr[i * size + offset:i * size + offset + 1, :], (size, LANES))
                 for i in range(N // size)]
        return parts[0] if len(parts) == 1 else jnp.concatenate(parts, axis=0)

    trow = lax.broadcasted_iota(jnp.int32, (N, LANES), 0)
    qb16 = q.astype(BF16)
    kb16 = kk.astype(BF16)
    a = jnp.where(row == colm, _dot_nt(qb16, kb16), 0.0)
    m = C // 2
    while m >= 1:
        size = 2 * m
        u = lax.bitwise_and(trow, size - 1)
        upper = u >= m
        if size >= 8:
            ref = gather_rows(size, m - 1)
        elif size == 4:
            ref = jnp.where(u == 0, pltpu.roll(b, N - 1, 0),
                            jnp.where(u == 1, b, jnp.where(u == 2, pltpu.roll(b, 1, 0), pltpu.roll(b, 2, 0))))
        else:
            ref = jnp.where(u == 1, pltpu.roll(b, 1, 0), b)
        x = jnp.exp(jnp.where(upper, b - ref, ref - b))
        qm = jnp.where(upper, q * x, 0.0).astype(BF16)
        km = jnp.where(upper, 0.0, kk * x).astype(BF16)
        a = a + jnp.where(same_block(size), _dot_nt(qm, km), 0.0)
        m //= 2

    vb16 = vi.astype(BF16)
    o = _dot(a.astype(BF16), vb16)

    qin = (q * jnp.exp(b)).astype(BF16)
    kd = kk * jnp.exp(gather_rows(C, C - 1) - b)
    vt = vi.T.astype(BF16)
    o_parts, st_new = [], []
    for gi in range(G):
        st = st_list[gi]
        o_parts.append(_dot_nt(qin[gi * C:(gi + 1) * C, :], st.astype(BF16)))
        seg = lax.shift_right_logical(trow, C.bit_length() - 1) == gi
        kd_g = jnp.where(seg, kd, 0.0).astype(BF16)
        ebl = jnp.exp(b_scr[(gi + 1) * C - 1:(gi + 1) * C, :])
        st_new.append(st * ebl + _dot(vt, kd_g))
    o = o + (o_parts[0] if G == 1 else jnp.concatenate(o_parts, axis=0))
    return o, st_new


def _gla_out(o, z, dn):
    ms = jnp.mean(o * o, axis=-1, keepdims=True)
    return (_silu(z) * (o * lax.rsqrt(ms + EPS) * dn)).astype(BF16)


def _hgrn_prompt_kernel(q_ref, f_ref, i_ref, z_ref, lbl_ref, dn_ref, y_ref, s_ref, st_scr, b_scr,
                        *, layer, nseq):
    step = pl.program_id(0)
    C = CHUNK

    @pl.when(step == 0)
    def _():
        st_scr[...] = jnp.zeros(st_scr.shape, F32)

    lb_row = _lower_bound(lbl_ref, layer)
    lb = jnp.concatenate([jnp.broadcast_to(lb_row[:, h * DH:(h + 1) * DH], (C, DH)) for h in range(H)], axis=0)
    dn = dn_ref[...]

    def stack(ref, r0):
        return jnp.concatenate([ref[r0:r0 + C, h * DH:(h + 1) * DH].astype(F32) for h in range(H)], axis=0)

    for cidx in range(nseq):
        r0 = cidx * C
        st_list = [st_scr[h] for h in range(H)]
        o, st_new = _gla_group(stack(q_ref, r0), stack(f_ref, r0), stack(i_ref, r0), lb, st_list, b_scr, C=C)
        y = _gla_out(o, stack(z_ref, r0), dn)
        for h in range(H):
            st_scr[h] = st_new[h]
            y_ref[r0:r0 + C, h * DH:(h + 1) * DH] = y[h * C:(h + 1) * C, :]

    @pl.when(step == pl.num_programs(0) - 1)
    def _():
        for h in range(H):
            s_ref[h] = st_scr[h].T


def hgrn_prompt(proj, lb_logits, d_norm, layer, *, nseq=4):
    T = proj.shape[0]
    rows = nseq * CHUNK

    def col(g):
        return pl.BlockSpec((rows, D_GRP), lambda i, g=g: (i, g))

    return pl.pallas_call(
        functools.partial(_hgrn_prompt_kernel, layer=layer, nseq=nseq),
        grid=(T // rows,),
        in_specs=[col(S_DQ), col(S_DF), col(S_DI), col(S_DZ),
                  pl.BlockSpec(lb_logits.shape, lambda i: (0, 0)),
                  pl.BlockSpec((1, DH), lambda i: (0, 0))],
        out_specs=[pl.BlockSpec((rows, D_GRP), lambda i: (i, 0)),
                   pl.BlockSpec((H, DH, DH), lambda i: (0, 0, 0))],
        out_shape=[jax.ShapeDtypeStruct((T, D_GRP), BF16),
                   jax.ShapeDtypeStruct((H, DH, DH), F32)],
        scratch_shapes=[pltpu.VMEM((H, DH, DH), F32), pltpu.VMEM((H * CHUNK, DH), F32)],
        compiler_params=_cparams("arbitrary"),
        name="hgrn_prompt",
    )(proj, proj, proj, proj, lb_logits, d_norm.reshape(1, DH))


def _hgrn_sample_kernel(q_ref, f_ref, i_ref, z_ref, s0_ref, lbl_ref, dn_ref, y_ref, s_ref, b_scr,
                        *, layer, nb, C):
    lb_row = _lower_bound(lbl_ref, layer)
    segs = [(bi, h) for bi in range(nb) for h in range(H)]
    lb = jnp.concatenate([jnp.broadcast_to(lb_row[:, h * DH:(h + 1) * DH], (C, DH)) for _, h in segs], axis=0)

    def stack(ref):
        return jnp.concatenate([ref[bi, :, h * DH:(h + 1) * DH].astype(F32) for bi, h in segs], axis=0)

    st_list = [s0_ref[bi, h].T for bi, h in segs]
    o, st_new = _gla_group(stack(q_ref), stack(f_ref), stack(i_ref), lb, st_list, b_scr, C=C)
    y = _gla_out(o, stack(z_ref), dn_ref[...])
    for gi, (bi, h) in enumerate(segs):
        s_ref[bi, h] = st_new[gi].T
        y_ref[bi, :, h * DH:(h + 1) * DH] = y[gi * C:(gi + 1) * C, :]


def hgrn_sample(proj3, state, lb_logits, d_norm, layer, *, nb=4):
    B, C, _ = proj3.shape

    def col(g):
        return pl.BlockSpec((nb, C, D_GRP), lambda i, g=g: (i, 0, g))

    return pl.pallas_call(
        functools.partial(_hgrn_sample_kernel, layer=layer, nb=nb, C=C),
        grid=(B // nb,),
        in_specs=[col(S_DQ), col(S_DF), col(S_DI), col(S_DZ),
                  pl.BlockSpec((None, nb, H, DH, DH), lambda i: (layer, i, 0, 0, 0)),
                  pl.BlockSpec(lb_logits.shape, lambda i: (0, 0)),
                  pl.BlockSpec((1, DH), lambda i: (0, 0))],
        out_specs=[pl.BlockSpec((nb, C, D_GRP), lambda i: (i, 0, 0)),
                   pl.BlockSpec((nb, H, DH, DH), lambda i: (i, 0, 0, 0))],
        out_shape=[jax.ShapeDtypeStruct((B, C, D_GRP), BF16),
                   jax.ShapeDtypeStruct((B, H, DH, DH), F32)],
        scratch_shapes=[pltpu.VMEM((nb * H * C, DH), F32)],
        compiler_params=_cparams("parallel"),
        name="hgrn_sample",
    )(proj3, proj3, proj3, proj3, state, lb_logits, d_norm.reshape(1, DH))


def _outproj_kernel(ya_ref, yb_ref, yc_ref, yd_ref, w_ref, x_ref, g_ref, fw_ref, o_ref, *, final):
    acc = _dot(ya_ref[...].reshape(-1, D_GRP), w_ref[0:D_GRP, :])
    for i, ref in enumerate((yb_ref, yc_ref, yd_ref), start=1):
        acc = acc + _dot(ref[...].reshape(-1, D_GRP), w_ref[i * D_GRP:(i + 1) * D_GRP, :])
    x = x_ref[...]
    xn = x + g_ref[...] * acc.reshape(x.shape)
    if final:
        ms = jnp.mean(xn * xn, axis=-1, keepdims=True)
        xn = xn * lax.rsqrt(ms + EPS) * fw_ref[...]
    o_ref[...] = xn


def out_proj(ya, yb, yc, yd, w_out, x, gate, final_w, *, bb, tt, final):
    B, T, D = x.shape
    nt = T // tt

    def ymap(i):
        return (i // nt, i % nt, 0)

    yspec = pl.BlockSpec((bb, tt, D_GRP), ymap)
    return pl.pallas_call(
        functools.partial(_outproj_kernel, final=final),
        grid=((B // bb) * nt,),
        in_specs=[yspec, yspec, yspec, yspec,
                  pl.BlockSpec(w_out.shape, lambda i: (0, 0)),
                  pl.BlockSpec((bb, tt, D), ymap),
                  pl.BlockSpec((bb, 1, D), lambda i: (i // nt, 0, 0)),
                  pl.BlockSpec((1, 1, D), lambda i: (0, 0, 0))],
        out_specs=pl.BlockSpec((bb, tt, D), ymap),
        out_shape=jax.ShapeDtypeStruct((B, T, D), F32),
        compiler_params=_cparams("parallel"),
        name="out_proj",
    )(ya, yb, yc, yd, w_out, x, gate, final_w.reshape(1, 1, D))


def kernel(x_prompt, x_sample, c_prompt, c_sample, cache_a_conv, cache_b_conv, cache_k, cache_v, cache_logf, state_hgrn, norm_w, w_ada, b_ada, w_in, b_f, a_conv_w, b_conv_w, b_conv_b, b_ln_w, b_ln_b, d_norm_w, hgrn_lb_logits, w_out, final_norm_w):
    L = DEPTH
    Bp, Tp, D = x_prompt.shape
    Bs, Ts, _ = x_sample.shape
    P = cache_k.shape[2]
    assert Bp == 1

    nf = 11 * D_GRP
    w_main = jnp.concatenate([w_in[:, :, :nf], w_in[:, :, nf + H:]], axis=-1).astype(BF16)
    w_f = jnp.pad(w_in[:, :, nf:nf + H], ((0, 0), (0, 0), (0, LANES - H))).astype(BF16)
    bf_pad = jnp.pad(b_f, ((0, 0), (0, LANES - H))).reshape(L, 1, LANES)
    w_out_b = w_out.astype(BF16)
    cache_k4 = cache_k.reshape(L, Bs, P, D_GRP)
    cache_v4 = cache_v.reshape(L, Bs, P, D_GRP)

    nc = Bp + Bs
    c_all = jnp.pad(jnp.concatenate([c_prompt, c_sample], axis=0), ((0, (-nc) % 8), (0, 0)))
    mod = ada_mod(c_all, w_ada, b_ada)

    hist_lf = jnp.transpose(cache_logf, (0, 1, 3, 2)).reshape(L * Bs * H, P)
    dkh_all = cumsum_lanes(hist_lf).reshape(L, Bs, H, P)

    xp, xs = x_prompt, x_sample
    zeros_a = jnp.zeros((Bp, A_CONV - 1, D_GRP), F32)
    zeros_b = jnp.zeros((Bp, B_CONV - 1, D_GRP), F32)
    outs_p = [[] for _ in range(6)]
    outs_s = [[] for _ in range(6)]
    for l in range(L):
        last = l == L - 1
        shift_p, scale_p, gate_p = (mod[l, :Bp, i * D:(i + 1) * D].reshape(Bp, 1, D) for i in range(3))
        shift_s, scale_s, gate_s = (mod[l, Bp:nc, i * D:(i + 1) * D].reshape(Bs, 1, D) for i in range(3))

        proj, k_p, v_p, lf_p = in_proj(xp, norm_w[l], scale_p, shift_p, w_main[l], w_f[l], bf_pad[l],
                                       bb=1, tt=IN_TM)
        ya, yb, na_p, nb_p = conv_ab(proj.reshape(Bp, Tp, -1), zeros_a, zeros_b, a_conv_w[l], b_conv_w[l],
                                     b_conv_b[l], b_ln_w[l], b_ln_b[l], tt=CONV_TT)
        lf_t = jnp.pad(lf_p[:, :H].T, ((0, 8 - H), (0, 0)))
        dcum = cumsum_lanes(lf_t)[:H]
        yc = fox_prompt(proj, dcum.reshape(H, Tp, 1), dcum.reshape(H, 1, Tp), tile=ATT_T)
        yd, s_p = hgrn_prompt(proj, hgrn_lb_logits, d_norm_w[l], l)
        xp = out_proj(ya, yb, yc.reshape(Bp, Tp, D_GRP), yd.reshape(Bp, Tp, D_GRP), w_out_b[l], xp, gate_p,
                      final_norm_w, bb=1, tt=OUT_TM, final=last)
        for i, a in enumerate((na_p, nb_p, k_p.reshape(Bp, Tp, H, DH), v_p.reshape(Bp, Tp, H, DH),
                               lf_p[:, :H].reshape(Bp, Tp, H), s_p.reshape(Bp, H, DH, DH))):
            outs_p[i].append(a)

        proj, k_s, v_s, lf_s = in_proj(xs, norm_w[l], scale_s, shift_s, w_main[l], w_f[l], bf_pad[l],
                                       bb=Bs, tt=Ts)
        proj3 = proj.reshape(Bs, Ts, -1)
        ya, yb, na_s, nb_s = conv_ab(proj3, cache_a_conv[l], cache_b_conv[l], a_conv_w[l], b_conv_w[l],
                                     b_conv_b[l], b_ln_w[l], b_ln_b[l], tt=Ts)
        lf_new = lf_s[:, :H].reshape(Bs, Ts, H)
        lf_rows = jnp.pad(jnp.transpose(lf_new, (0, 2, 1)).reshape(Bs * H, Ts), ((0, 0), (0, LANES - Ts)))
        dkh = dkh_all[l]
        dnew = cumsum_lanes(lf_rows)[:, :Ts].reshape(Bs, H, Ts) + dkh[:, :, P - 1:P]
        yc = fox_sample(proj3, cache_k4, cache_v4, l, dkh, jnp.transpose(dnew, (0, 2, 1)), dnew,
                        tk=ATT_TK_CACHE)
        yd, s_s = hgrn_sample(proj3, state_hgrn, hgrn_lb_logits, d_norm_w[l], l)
        xs = out_proj(ya, yb, yc, yd, w_out_b[l], xs, gate_s, final_norm_w, bb=Bs, tt=Ts, final=last)
        for i, a in enumerate((na_s, nb_s, k_s.reshape(Bs, Ts, H, DH), v_s.reshape(Bs, Ts, H, DH),
                               lf_new, s_s)):
            outs_s[i].append(a)

    res = [xp, xs]
    for i in range(6):
        res.append(jnp.stack(outs_p[i]))
        res.append(jnp.stack(outs_s[i]))
    return tuple(res)
```

```python
import functools

import jax
import jax.numpy as jnp
from jax import lax
from jax.experimental import pallas as pl
from jax.experimental.pallas import tpu as pltpu

F32 = jnp.float32
BF16 = jnp.bfloat16

DEPTH = 4
D_MODEL = 2048
D_GRP = 512
N_SPLIT = 15
H = 4
DH = 128
A_CONV = 3
B_CONV = 31
CHUNK = 64
EPS = 1e-6
LANES = 128
VMEM_LIMIT = 56 * 1024 * 1024

(S_AB, S_AC, S_AX, S_AZ, S_BA, S_BG, S_BZ, S_CQ, S_CK, S_CV, S_CZ, S_DQ, S_DF, S_DI, S_DZ) = range(15)

IN_TN = 1536
IN_TM = 512
CONV_TT = 256
ATT_T = 512
ATT_TK_CACHE = 1024
OUT_TM = 512


def _cparams(*sem):
    return pltpu.CompilerParams(dimension_semantics=sem, vmem_limit_bytes=VMEM_LIMIT)


def _sigmoid(x):
    return 1.0 / (1.0 + jnp.exp(-x))


def _silu(x):
    return x * _sigmoid(x)


def _log_sigmoid(x):
    return jnp.minimum(x, 0.0) - jnp.log(1.0 + jnp.exp(-jnp.abs(x)))


def _split3(x):
    hi = x.astype(BF16)
    r1 = x - hi.astype(F32)
    mid = r1.astype(BF16)
    lo = (r1 - mid.astype(F32)).astype(BF16)
    return hi, mid, lo


def _dot(a, b):
    return jnp.dot(a, b, preferred_element_type=F32)


def _dot_nt(a, b):
    return lax.dot_general(a, b, (((1,), (1,)), ((), ())), preferred_element_type=F32)


def _ada_kernel(c_ref, w_ref, b_ref, o_ref):
    c = c_ref[...]
    a = _silu(c).astype(BF16)
    o_ref[...] = _dot(a, w_ref[...].astype(BF16)) + b_ref[...]


def ada_mod(c_all, w_ada, b_ada):
    L, D, N = w_ada.shape
    R = c_all.shape[0]
    tn = 768
    return pl.pallas_call(
        _ada_kernel,
        grid=(L, N // tn),
        in_specs=[pl.BlockSpec((R, D), lambda l, j: (0, 0)),
                  pl.BlockSpec((None, D, tn), lambda l, j: (l, 0, j)),
                  pl.BlockSpec((None, 1, tn), lambda l, j: (l, 0, j))],
        out_specs=pl.BlockSpec((None, R, tn), lambda l, j: (l, 0, j)),
        out_shape=jax.ShapeDtypeStruct((L, R, N), F32),
        compiler_params=_cparams("parallel", "parallel"),
        name="ada_mod",
    )(c_all, w_ada, b_ada.reshape(L, 1, N))


def _inproj_kernel(x_ref, nw_ref, sc_ref, sh_ref, w_ref, wf_ref, bf_ref,
                   proj_ref, k_ref, v_ref, logf_ref, h_scr):
    j = pl.program_id(1)

    @pl.when(j == 0)
    def _():
        x = x_ref[...]
        ms = jnp.mean(x * x, axis=-1, keepdims=True)
        y = x * lax.rsqrt(ms + EPS) * nw_ref[...]
        h = y * (1.0 + sc_ref[...]) + sh_ref[...]
        hb = h.reshape(h_scr.shape).astype(BF16)
        h_scr[...] = hb
        cf = _dot(hb, wf_ref[...]) + bf_ref[...]
        logf_ref[...] = _log_sigmoid(cf)

    h = h_scr[...]
    for c in range(IN_TN // D_GRP):
        r = _dot(h, w_ref[:, c * D_GRP:(c + 1) * D_GRP])
        proj_ref[:, c * D_GRP:(c + 1) * D_GRP] = r.astype(BF16)
        for split, ref in ((S_CK, k_ref), (S_CV, v_ref)):
            if split % 3 == c:
                @pl.when(j == split // 3)
                def _(r=r, ref=ref):
                    ref[...] = r


def in_proj(x, norm_w, scale, shift, w_main, w_f, b_f, *, bb, tt):
    B, T, D = x.shape
    rows = bb * tt
    nb, nt = B // bb, T // tt
    R = B * T
    N = w_main.shape[1]

    def xmap(i, j):
        return (i // nt, i % nt, 0)

    def mmap(i, j):
        return (i // nt, 0, 0)

    return pl.pallas_call(
        _inproj_kernel,
        grid=(nb * nt, N // IN_TN),
        in_specs=[pl.BlockSpec((bb, tt, D), xmap),
                  pl.BlockSpec((1, 1, D), lambda i, j: (0, 0, 0)),
                  pl.BlockSpec((bb, 1, D), mmap),
                  pl.BlockSpec((bb, 1, D), mmap),
                  pl.BlockSpec((D, IN_TN), lambda i, j: (0, j)),
                  pl.BlockSpec((D, LANES), lambda i, j: (0, 0)),
                  pl.BlockSpec((1, LANES), lambda i, j: (0, 0))],
        out_specs=[pl.BlockSpec((rows, IN_TN), lambda i, j: (i, j)),
                   pl.BlockSpec((rows, D_GRP), lambda i, j: (i, 0)),
                   pl.BlockSpec((rows, D_GRP), lambda i, j: (i, 0)),
                   pl.BlockSpec((rows, LANES), lambda i, j: (i, 0))],
        out_shape=[jax.ShapeDtypeStruct((R, N), BF16),
                   jax.ShapeDtypeStruct((R, D_GRP), F32),
                   jax.ShapeDtypeStruct((R, D_GRP), F32),
                   jax.ShapeDtypeStruct((R, LANES), F32)],
        scratch_shapes=[pltpu.VMEM((rows, D), BF16)],
        compiler_params=_cparams("parallel", "arbitrary"),
        name="in_proj",
    )(x, norm_w.reshape(1, 1, D), scale, shift, w_main, w_f, b_f)


HA = 8
HB = 32

def _conv_kernel(ab_ref, ac_ref, ax_ref, az_ref, ba_ref, bg_ref, bz_ref, ha_ref, hb_ref,
                 wa_ref, wb_ref, bb_ref, lw_ref, lb_ref,
                 ya_ref, yb_ref, na_ref, nb_ref, seqa, seqb, *, tt):
    t = pl.program_id(1)
    na, nb = A_CONV - 1, B_CONV - 1

    @pl.when(t == 0)
    def _():
        seqa[HA - na:HA, :] = ha_ref[0]
        seqb[0, 0:HB - nb, :] = jnp.zeros((HB - nb, D_GRP), F32)
        seqb[0, HB - nb:HB, :] = hb_ref[0]

    @pl.when(t > 0)
    def _():
        ta = seqa[HA + tt - na:HA + tt, :]
        seqa[HA - na:HA, :] = ta
        tb = seqb[0, HB + tt - nb:HB + tt, :]
        seqb[0, HB - nb:HB, :] = tb

    seqa[HA:HA + tt, :] = ac_ref[0].astype(F32) * ax_ref[0].astype(F32)
    seqb[0, HB:HB + tt, :] = ba_ref[0].astype(F32) * _sigmoid(bg_ref[0].astype(F32))
    for r in range(1, 8):
        seqb[r, 0:HB + tt - r, :] = seqb[0, r:HB + tt, :]

    rc = min(tt, 32)
    for r0 in range(0, tt, rc):
        acc = wa_ref[0:1, :] * seqa[HA - na + r0:HA - na + r0 + rc, :]
        for w in range(1, A_CONV):
            acc = acc + wa_ref[w:w + 1, :] * seqa[HA - na + r0 + w:HA - na + r0 + w + rc, :]
        ya = _silu(az_ref[0, r0:r0 + rc, :].astype(F32)) * ab_ref[0, r0:r0 + rc, :].astype(F32) * acc
        ya_ref[0, r0:r0 + rc, :] = ya.astype(BF16)

        acc = None
        for w in range(B_CONV):
            a8, r8 = divmod(HB - nb + w, 8)
            term = wb_ref[w:w + 1, :] * seqb[r8, 8 * a8 + r0:8 * a8 + r0 + rc, :]
            acc = term if acc is None else acc + term
        y = acc + bb_ref[...]
        mu = jnp.mean(y, axis=-1, keepdims=True)
        yc = y - mu
        var = jnp.mean(yc * yc, axis=-1, keepdims=True)
        yn = yc * lax.rsqrt(var + EPS) * lw_ref[...] + lb_ref[...]
        yb = _silu(bz_ref[0, r0:r0 + rc, :].astype(F32)) * _silu(yn)
        yb_ref[0, r0:r0 + rc, :] = yb.astype(BF16)

    na_ref[0] = seqa[HA + tt - na:HA + tt, :]
    nb_ref[0] = seqb[0, HB + tt - nb:HB + tt, :]


def conv_ab(proj3, hist_a, hist_b, wa, wb, bias_b, ln_w, ln_b, *, tt):
    B, T, _ = proj3.shape
    nt = T // tt

    def col(g):
        return pl.BlockSpec((1, tt, D_GRP), lambda b, t, g=g: (b, t, g))

    def full(a):
        return pl.BlockSpec(a.shape, lambda b, t: (0,) * a.ndim)

    bias_b, ln_w, ln_b = (a.reshape(1, D_GRP) for a in (bias_b, ln_w, ln_b))
    return pl.pallas_call(
        functools.partial(_conv_kernel, tt=tt),
        grid=(B, nt),
        in_specs=[col(S_AB), col(S_AC), col(S_AX), col(S_AZ), col(S_BA), col(S_BG), col(S_BZ),
                  pl.BlockSpec((1, A_CONV - 1, D_GRP), lambda b, t: (b, 0, 0)),
                  pl.BlockSpec((1, B_CONV - 1, D_GRP), lambda b, t: (b, 0, 0)),
                  full(wa), full(wb), full(bias_b), full(ln_w), full(ln_b)],
        out_specs=[pl.BlockSpec((1, tt, D_GRP), lambda b, t: (b, t, 0)),
                   pl.BlockSpec((1, tt, D_GRP), lambda b, t: (b, t, 0)),
                   pl.BlockSpec((1, A_CONV - 1, D_GRP), lambda b, t: (b, 0, 0)),
                   pl.BlockSpec((1, B_CONV - 1, D_GRP), lambda b, t: (b, 0, 0))],
        out_shape=[jax.ShapeDtypeStruct((B, T, D_GRP), BF16),
                   jax.ShapeDtypeStruct((B, T, D_GRP), BF16),
                   jax.ShapeDtypeStruct((B, A_CONV - 1, D_GRP), F32),
                   jax.ShapeDtypeStruct((B, B_CONV - 1, D_GRP), F32)],
        scratch_shapes=[pltpu.VMEM((HA + tt, D_GRP), F32), pltpu.VMEM((8, HB + tt, D_GRP), F32)],
        compiler_params=_cparams("parallel", "arbitrary"),
        name="conv_ab",
    )(proj3, proj3, proj3, proj3, proj3, proj3, proj3, hist_a, hist_b, wa, wb, bias_b, ln_w, ln_b)


def _cumsum_kernel(x_ref, o_ref, carry, *, cb, stride):
    j = pl.program_id(0)

    @pl.when(j == 0)
    def _():
        carry[...] = jnp.zeros_like(carry)

    lane = lax.broadcasted_iota(jnp.int32, carry.shape, 1)
    head = jnp.where(lane < stride, pltpu.roll(carry[...], stride, 1), 0.0)
    x = x_ref[...]
    x = jnp.concatenate([x[:, 0:LANES] + head, x[:, LANES:]], axis=1) if cb > LANES else x + head
    hi, mid, lo = _split3(x)
    r = lax.broadcasted_iota(jnp.int32, (cb, cb), 0)
    c = lax.broadcasted_iota(jnp.int32, (cb, cb), 1)
    same_series = lax.bitwise_and(r, stride - 1) == lax.bitwise_and(c, stride - 1)
    u = ((r <= c) & same_series).astype(BF16)
    cs = _dot(hi, u) + _dot(mid, u) + _dot(lo, u)
    o_ref[...] = cs
    carry[...] = cs[:, cb - LANES:cb]


def cumsum_lanes(x, stride=1):
    R, L = x.shape
    cb = 256 if L % 256 == 0 else LANES
    return pl.pallas_call(
        functools.partial(_cumsum_kernel, cb=cb, stride=stride),
        grid=(L // cb,),
        in_specs=[pl.BlockSpec((R, cb), lambda j: (0, j))],
        out_specs=pl.BlockSpec((R, cb), lambda j: (0, j)),
        out_shape=jax.ShapeDtypeStruct((R, L), F32),
        scratch_shapes=[pltpu.VMEM((R, LANES), F32)],
        compiler_params=_cparams("arbitrary"),
        name="cumsum_lanes",
    )(x)


LOG2E = 1.4426950408889634
ATT_PREP_ROWS = 1024
ATT_SPLIT = 2


def _bias_lanes(d, key_side):
    hi, mid, lo = (p.astype(F32) for p in _split3(d))
    lane = lax.broadcasted_iota(jnp.int32, (d.shape[0], LANES), 1)
    if key_side:
        a = jnp.where(lane == 3, -hi, jnp.where(lane == 4, -mid, jnp.where(lane == 5, -lo, 0.0)))
        a = jnp.where(lane < 3, 1.0, a)
    else:
        a = jnp.where(lane == 0, hi, jnp.where(lane == 1, mid, jnp.where(lane == 2, lo, 0.0)))
        a = jnp.where((lane >= 3) & (lane < 6), 1.0, a)
    return a.astype(BF16)


def _fox_prompt_kernel(q_ref, k_ref, v_ref, z_ref, dq_ref, dk_ref, y_ref,
                       ka_scr, va_scr, qa_scr, s0_scr, s1_scr, p_scr, alpha_scr, m_scr, l_scr, acc_scr, *, tile):
    qi = pl.program_id(1)
    T = k_ref.shape[0]
    th = tile // ATT_SPLIT

    @pl.when(qi == 0)
    def _():
        pr = min(T, ATT_PREP_ROWS)
        one_lane = (lax.broadcasted_iota(jnp.int32, (pr, LANES), 1) == 0).astype(BF16)

        def prep(i, _):
            rows = pl.ds(pl.multiple_of(i * pr, pr), pr)
            ka_scr[rows, 0:DH] = k_ref[rows, :]
            ka_scr[rows, DH:2 * DH] = _bias_lanes(dk_ref[rows, :] * LOG2E, True)
            va_scr[rows, 0:DH] = v_ref[rows, :]
            va_scr[rows, DH:2 * DH] = one_lane
            return 0

        lax.fori_loop(0, T // pr, prep, 0)

    qa_scr[:, 0:DH] = (q_ref[...].astype(F32) * (DH ** -0.5 * LOG2E)).astype(BF16)
    qa_scr[:, DH:2 * DH] = _bias_lanes(dq_ref[...] * LOG2E, False)

    m_scr[...] = jnp.full(m_scr.shape, -jnp.inf, F32)
    l_scr[...] = jnp.zeros(l_scr.shape, F32)
    acc_scr[...] = jnp.zeros(acc_scr.shape, F32)

    def logits(ki, s_buf):
        off = pl.multiple_of(ki * tile, tile)
        ka = ka_scr[pl.ds(off, tile), :]
        for g in range(ATT_SPLIT):
            rows = slice(g * th, (g + 1) * th)
            s_buf[rows, :] = _dot_nt(qa_scr[rows, :], ka)

    def softmax(s_buf, masked):
        for g in range(ATT_SPLIT):
            rows = slice(g * th, (g + 1) * th)
            s = s_buf[rows, :]
            if masked:
                r = lax.broadcasted_iota(jnp.int32, (th, tile), 0) + g * th
                c = lax.broadcasted_iota(jnp.int32, (th, tile), 1)
                s = jnp.where(c <= r, s, -jnp.inf)
            m = m_scr[rows, :]
            m_new = jnp.maximum(m, jnp.max(s, axis=-1, keepdims=True))
            alpha_scr[rows, :] = jnp.exp2(m - m_new)
            p_scr[rows, :] = jnp.exp2(s - m_new).astype(BF16)
            m_scr[rows, :] = m_new

    def weighted_values(ki):
        off = pl.multiple_of(ki * tile, tile)
        va = va_scr[pl.ds(off, tile), :]
        for g in range(ATT_SPLIT):
            rows = slice(g * th, (g + 1) * th)
            pv = _dot(p_scr[rows, :], va)
            alpha = alpha_scr[rows, :]
            l_scr[rows, :] = alpha * l_scr[rows, :] + pv[:, DH:DH + 1]
            acc_scr[rows, :] = alpha * acc_scr[rows, :] + pv[:, 0:DH]

    p_scr[...] = jnp.zeros(p_scr.shape, BF16)
    alpha_scr[...] = jnp.ones(alpha_scr.shape, F32)
    s_bufs = (s0_scr, s1_scr)
    logits(0, s_bufs[0])

    def stage(ki, parity, masked):
        weighted_values(jnp.maximum(ki - 1, 0))
        softmax(s_bufs[parity], masked)
        if not masked:
            logits(ki + 1, s_bufs[1 - parity])

    def pair(j, _):
        stage(2 * j, 0, False)
        stage(2 * j + 1, 1, False)
        return 0

    lax.fori_loop(0, lax.shift_right_logical(qi, 1), pair, 0)
    odd = lax.bitwise_and(qi, 1) == 1

    @pl.when(odd)
    def _():
        stage(qi - 1, 0, False)
        stage(qi, 1, True)

    @pl.when(jnp.logical_not(odd))
    def _():
        stage(qi, 0, True)

    weighted_values(qi)
    y_ref[...] = (_silu(z_ref[...].astype(F32)) * (acc_scr[...] / l_scr[...])).astype(BF16)


def fox_prompt(proj, dcol, *, tile):
    T = proj.shape[0]
    cpb = D_GRP // DH
    return pl.pallas_call(
        functools.partial(_fox_prompt_kernel, tile=tile),
        grid=(H, T // tile),
        in_specs=[pl.BlockSpec((tile, DH), lambda h, i: (i, S_CQ * cpb + h)),
                  pl.BlockSpec((T, DH), lambda h, i: (0, S_CK * cpb + h)),
                  pl.BlockSpec((T, DH), lambda h, i: (0, S_CV * cpb + h)),
                  pl.BlockSpec((tile, DH), lambda h, i: (i, S_CZ * cpb + h)),
                  pl.BlockSpec((None, tile, 1), lambda h, i: (h, i, 0)),
                  pl.BlockSpec((None, T, 1), lambda h, i: (h, 0, 0))],
        out_specs=pl.BlockSpec((tile, DH), lambda h, i: (i, h)),
        out_shape=jax.ShapeDtypeStruct((T, D_GRP), BF16),
        scratch_shapes=[pltpu.VMEM((T, 2 * DH), BF16), pltpu.VMEM((T, 2 * DH), BF16),
                        pltpu.VMEM((tile, 2 * DH), BF16), pltpu.VMEM((tile, tile), F32),
                        pltpu.VMEM((tile, tile), F32), pltpu.VMEM((tile, tile), BF16), pltpu.VMEM((tile, 1), F32),
                        pltpu.VMEM((tile, 1), F32), pltpu.VMEM((tile, 1), F32), pltpu.VMEM((tile, DH), F32)],
        compiler_params=_cparams("parallel", "arbitrary"),
        name="fox_prompt",
    )(proj, proj, proj, proj, dcol, dcol)


def _fox_sample_kernel(q_ref, kn_ref, vn_ref, z_ref, kc_ref, vc_ref, dkh_ref, dq_ref, dkn_ref,
                       y_ref, m_scr, l_scr, acc_scr, *, tq):
    kt = pl.program_id(1)
    nk = pl.num_programs(1)
    scale = DH ** -0.5

    @pl.when(kt == 0)
    def _():
        m_scr[...] = jnp.full(m_scr.shape, -jnp.inf, F32)
        l_scr[...] = jnp.zeros(l_scr.shape, F32)
        acc_scr[...] = jnp.zeros(acc_scr.shape, F32)

    def update(rows, s, v):
        m = m_scr[rows, :]
        m_new = jnp.maximum(m, jnp.max(s, axis=-1, keepdims=True))
        alpha = jnp.exp(m - m_new)
        p = jnp.exp(s - m_new)
        l_scr[rows, :] = alpha * l_scr[rows, :] + jnp.sum(p, axis=-1, keepdims=True)
        acc_scr[rows, :] = alpha * acc_scr[rows, :] + _dot(p.astype(BF16), v)
        m_scr[rows, :] = m_new

    heads = [slice(h * DH, (h + 1) * DH) for h in range(H)]
    q_stack = jnp.concatenate([q_ref[0, :, cs] for cs in heads], axis=0)
    dq_stack = jnp.concatenate([dq_ref[0, :, h:h + 1] for h in range(H)], axis=0)
    kf = kc_ref[0].astype(BF16)
    n = kf.shape[0]
    s = _dot_nt(q_stack, kf) * scale + (dq_stack - dkh_ref[0])
    row_head = lax.shift_right_logical(lax.broadcasted_iota(jnp.int32, (H * tq, n), 0), tq.bit_length() - 1)
    col_head = lax.bitwise_and(lax.broadcasted_iota(jnp.int32, (H * tq, n), 1), H - 1)
    update(slice(None), jnp.where(row_head == col_head, s, -jnp.inf), vc_ref[0].astype(BF16))

    @pl.when(kt == nk - 1)
    def _():
        r = lax.broadcasted_iota(jnp.int32, (tq, tq), 0)
        c = lax.broadcasted_iota(jnp.int32, (tq, tq), 1)
        for h in range(H):
            rows = slice(h * tq, (h + 1) * tq)
            s_new = _dot_nt(q_ref[0, :, heads[h]], kn_ref[0, :, heads[h]]) * scale \
                + (dq_ref[0, :, h:h + 1] - dkn_ref[0, h:h + 1, :])
            update(rows, jnp.where(c <= r, s_new, -jnp.inf), vn_ref[0, :, heads[h]])
            o = acc_scr[rows, :] / l_scr[rows, :]
            y_ref[0, :, heads[h]] = (_silu(z_ref[0, :, heads[h]].astype(F32)) * o).astype(BF16)


def fox_sample(proj3, cache_k, cache_v, layer, dkh, dq_col, dkn_row, *, tk):
    B, tq, _ = proj3.shape
    n = tk * H
    assert tq & (tq - 1) == 0 and H & (H - 1) == 0

    def col(g):
        return pl.BlockSpec((1, tq, D_GRP), lambda b, t, g=g: (b, 0, g))

    cache_spec = pl.BlockSpec((None, 1, n, DH), lambda b, t: (layer, b, t, 0))
    return pl.pallas_call(
        functools.partial(_fox_sample_kernel, tq=tq),
        grid=(B, cache_k.shape[2] // n),
        in_specs=[col(S_CQ), col(S_CK), col(S_CV), col(S_CZ), cache_spec, cache_spec,
                  pl.BlockSpec((None, 1, 1, n), lambda b, t: (layer, b, 0, t)),
                  pl.BlockSpec((1, tq, H), lambda b, t: (b, 0, 0)),
                  pl.BlockSpec((1, H, tq), lambda b, t: (b, 0, 0))],
        out_specs=pl.BlockSpec((1, tq, D_GRP), lambda b, t: (b, 0, 0)),
        out_shape=jax.ShapeDtypeStruct((B, tq, D_GRP), BF16),
        scratch_shapes=[pltpu.VMEM((H * tq, 1), F32), pltpu.VMEM((H * tq, 1), F32),
                        pltpu.VMEM((H * tq, DH), F32)],
        compiler_params=_cparams("parallel", "arbitrary"),
        name="fox_sample",
    )(proj3, proj3, proj3, proj3, cache_k, cache_v, dkh, dq_col, dkn_row)


def _lower_bound(lbl_ref, layer):
    x = lbl_ref[...]
    e = jnp.exp(x - jnp.max(x, axis=0, keepdims=True))
    sm = e / jnp.sum(e, axis=0, keepdims=True)
    cum = sm[0:1, :]
    first = cum
    for i in range(1, layer + 1):
        cum = cum + sm[i:i + 1, :]
    return cum - first


def _gla_group(q, gl, vi, lb, st_list, b_scr, *, C):
    G = len(st_list)
    N = G * C
    sig = _sigmoid(gl)
    g = jnp.log(lb + (1.0 - lb) * sig)
    kk = (1.0 - lb) * _sigmoid(-gl)

    row = lax.broadcasted_iota(jnp.int32, (N, N), 0)
    colm = lax.broadcasted_iota(jnp.int32, (N, N), 1)

    def same_block(size):
        sh = size.bit_length() - 1
        return lax.shift_right_logical(row, sh) == lax.shift_right_logical(colm, sh)

    tril = (same_block(C) & (colm <= row)).astype(BF16)
    g_hi, g_mid, g_lo = _split3(g)
    b = _dot(tril, g_hi) + _dot(tril, g_mid) + _dot(tril, g_lo)
    b_scr[...] = b

    def gather_rows(size, offset):
        parts = [jnp.broadcast_to(b_scr[i * size + offset:i * size + offset + 1, :], (size, LANES))
                 for i in range(N // size)]
        return parts[0] if len(parts) == 1 else jnp.concatenate(parts, axis=0)

    trow = lax.broadcasted_iota(jnp.int32, (N, LANES), 0)
    qb16 = q.astype(BF16)
    kb16 = kk.astype(BF16)
    a = jnp.where(row == colm, _dot_nt(qb16, kb16), 0.0)
    m = C // 2
    while m >= 1:
        size = 2 * m
        u = lax.bitwise_and(trow, size - 1)
        upper = u >= m
        if size >= 8:
            ref = gather_rows(size, m - 1)
        elif size == 4:
            ref = jnp.where(u == 0, pltpu.roll(b, N - 1, 0),
                            jnp.where(u == 1, b, jnp.where(u == 2, pltpu.roll(b, 1, 0), pltpu.roll(b, 2, 0))))
        else:
            ref = jnp.where(u == 1, pltpu.roll(b, 1, 0), b)
        x = jnp.exp(jnp.where(upper, b - ref, ref - b))
        qm = jnp.where(upper, q * x, 0.0).astype(BF16)
        km = jnp.where(upper, 0.0, kk * x).astype(BF16)
        a = a + jnp.where(same_block(size), _dot_nt(qm, km), 0.0)
        m //= 2

    vb16 = vi.astype(BF16)
    o = _dot(a.astype(BF16), vb16)

    qin = (q * jnp.exp(b)).astype(BF16)
    kd = kk * jnp.exp(gather_rows(C, C - 1) - b)
    vt = vi.T.astype(BF16)
    o_parts, st_new = [], []
    for gi in range(G):
        st = st_list[gi]
        o_parts.append(_dot_nt(qin[gi * C:(gi + 1) * C, :], st.astype(BF16)))
        seg = lax.shift_right_logical(trow, C.bit_length() - 1) == gi
        kd_g = jnp.where(seg, kd, 0.0).astype(BF16)
        ebl = jnp.exp(b_scr[(gi + 1) * C - 1:(gi + 1) * C, :])
        st_new.append(st * ebl + _dot(vt, kd_g))
    o = o + (o_parts[0] if G == 1 else jnp.concatenate(o_parts, axis=0))
    return o, st_new


def _gla_out(o, z, dn):
    ms = jnp.mean(o * o, axis=-1, keepdims=True)
    return (_silu(z) * (o * lax.rsqrt(ms + EPS) * dn)).astype(BF16)


def _hgrn_prompt_kernel(q_ref, f_ref, i_ref, z_ref, lbl_ref, dn_ref, y_ref, s_ref, st_scr, b_scr,
                        *, layer, nseq):
    step = pl.program_id(0)
    C = CHUNK

    @pl.when(step == 0)
    def _():
        st_scr[...] = jnp.zeros(st_scr.shape, F32)

    lb_row = _lower_bound(lbl_ref, layer)
    lb = jnp.concatenate([jnp.broadcast_to(lb_row[:, h * DH:(h + 1) * DH], (C, DH)) for h in range(H)], axis=0)
    dn = dn_ref[...]

    def stack(ref, r0):
        return jnp.concatenate([ref[r0:r0 + C, h * DH:(h + 1) * DH].astype(F32) for h in range(H)], axis=0)

    for cidx in range(nseq):
        r0 = cidx * C
        st_list = [st_scr[h] for h in range(H)]
        o, st_new = _gla_group(stack(q_ref, r0), stack(f_ref, r0), stack(i_ref, r0), lb, st_list, b_scr, C=C)
        y = _gla_out(o, stack(z_ref, r0), dn)
        for h in range(H):
            st_scr[h] = st_new[h]
            y_ref[r0:r0 + C, h * DH:(h + 1) * DH] = y[h * C:(h + 1) * C, :]

    @pl.when(step == pl.num_programs(0) - 1)
    def _():
        for h in range(H):
            s_ref[h] = st_scr[h].T


def hgrn_prompt(proj, lb_logits, d_norm, layer, *, nseq=4):
    T = proj.shape[0]
    rows = nseq * CHUNK

    def col(g):
        return pl.BlockSpec((rows, D_GRP), lambda i, g=g: (i, g))

    return pl.pallas_call(
        functools.partial(_hgrn_prompt_kernel, layer=layer, nseq=nseq),
        grid=(T // rows,),
        in_specs=[col(S_DQ), col(S_DF), col(S_DI), col(S_DZ),
                  pl.BlockSpec(lb_logits.shape, lambda i: (0, 0)),
                  pl.BlockSpec((1, DH), lambda i: (0, 0))],
        out_specs=[pl.BlockSpec((rows, D_GRP), lambda i: (i, 0)),
                   pl.BlockSpec((H, DH, DH), lambda i: (0, 0, 0))],
        out_shape=[jax.ShapeDtypeStruct((T, D_GRP), BF16),
                   jax.ShapeDtypeStruct((H, DH, DH), F32)],
        scratch_shapes=[pltpu.VMEM((H, DH, DH), F32), pltpu.VMEM((H * CHUNK, DH), F32)],
        compiler_params=_cparams("arbitrary"),
        name="hgrn_prompt",
    )(proj, proj, proj, proj, lb_logits, d_norm.reshape(1, DH))


def _hgrn_sample_kernel(q_ref, f_ref, i_ref, z_ref, s0_ref, lbl_ref, dn_ref, y_ref, s_ref, b_scr,
                        *, layer, nb, C):
    lb_row = _lower_bound(lbl_ref, layer)
    segs = [(bi, h) for bi in range(nb) for h in range(H)]
    lb = jnp.concatenate([jnp.broadcast_to(lb_row[:, h * DH:(h + 1) * DH], (C, DH)) for _, h in segs], axis=0)

    def stack(ref):
        return jnp.concatenate([ref[bi, :, h * DH:(h + 1) * DH].astype(F32) for bi, h in segs], axis=0)

    st_list = [s0_ref[bi, h].T for bi, h in segs]
    o, st_new = _gla_group(stack(q_ref), stack(f_ref), stack(i_ref), lb, st_list, b_scr, C=C)
    y = _gla_out(o, stack(z_ref), dn_ref[...])
    for gi, (bi, h) in enumerate(segs):
        s_ref[bi, h] = st_new[gi].T
        y_ref[bi, :, h * DH:(h + 1) * DH] = y[gi * C:(gi + 1) * C, :]


def hgrn_sample(proj3, state, lb_logits, d_norm, layer, *, nb=4):
    B, C, _ = proj3.shape

    def col(g):
        return pl.BlockSpec((nb, C, D_GRP), lambda i, g=g: (i, 0, g))

    return pl.pallas_call(
        functools.partial(_hgrn_sample_kernel, layer=layer, nb=nb, C=C),
        grid=(B // nb,),
        in_specs=[col(S_DQ), col(S_DF), col(S_DI), col(S_DZ),
                  pl.BlockSpec((None, nb, H, DH, DH), lambda i: (layer, i, 0, 0, 0)),
                  pl.BlockSpec(lb_logits.shape, lambda i: (0, 0)),
                  pl.BlockSpec((1, DH), lambda i: (0, 0))],
        out_specs=[pl.BlockSpec((nb, C, D_GRP), lambda i: (i, 0, 0)),
                   pl.BlockSpec((nb, H, DH, DH), lambda i: (i, 0, 0, 0))],
        out_shape=[jax.ShapeDtypeStruct((B, C, D_GRP), BF16),
                   jax.ShapeDtypeStruct((B, H, DH, DH), F32)],
        scratch_shapes=[pltpu.VMEM((nb * H * C, DH), F32)],
        compiler_params=_cparams("parallel"),
        name="hgrn_sample",
    )(proj3, proj3, proj3, proj3, state, lb_logits, d_norm.reshape(1, DH))


def _outproj_kernel(ya_ref, yb_ref, yc_ref, yd_ref, w_ref, x_ref, g_ref, fw_ref, o_ref, *, final):
    acc = _dot(ya_ref[...].reshape(-1, D_GRP), w_ref[0:D_GRP, :])
    for i, ref in enumerate((yb_ref, yc_ref, yd_ref), start=1):
        acc = acc + _dot(ref[...].reshape(-1, D_GRP), w_ref[i * D_GRP:(i + 1) * D_GRP, :])
    x = x_ref[...]
    xn = x + g_ref[...] * acc.reshape(x.shape)
    if final:
        ms = jnp.mean(xn * xn, axis=-1, keepdims=True)
        xn = xn * lax.rsqrt(ms + EPS) * fw_ref[...]
    o_ref[...] = xn


def out_proj(ya, yb, yc, yd, w_out, x, gate, final_w, *, bb, tt, final):
    B, T, D = x.shape
    nt = T // tt

    def ymap(i):
        return (i // nt, i % nt, 0)

    yspec = pl.BlockSpec((bb, tt, D_GRP), ymap)
    return pl.pallas_call(
        functools.partial(_outproj_kernel, final=final),
        grid=((B // bb) * nt,),
        in_specs=[yspec, yspec, yspec, yspec,
                  pl.BlockSpec(w_out.shape, lambda i: (0, 0)),
                  pl.BlockSpec((bb, tt, D), ymap),
                  pl.BlockSpec((bb, 1, D), lambda i: (i // nt, 0, 0)),
                  pl.BlockSpec((1, 1, D), lambda i: (0, 0, 0))],
        out_specs=pl.BlockSpec((bb, tt, D), ymap),
        out_shape=jax.ShapeDtypeStruct((B, T, D), F32),
        compiler_params=_cparams("parallel"),
        name="out_proj",
    )(ya, yb, yc, yd, w_out, x, gate, final_w.reshape(1, 1, D))


def kernel(x_prompt, x_sample, c_prompt, c_sample, cache_a_conv, cache_b_conv, cache_k, cache_v, cache_logf, state_hgrn, norm_w, w_ada, b_ada, w_in, b_f, a_conv_w, b_conv_w, b_conv_b, b_ln_w, b_ln_b, d_norm_w, hgrn_lb_logits, w_out, final_norm_w):
    L = DEPTH
    Bp, Tp, D = x_prompt.shape
    Bs, Ts, _ = x_sample.shape
    P = cache_k.shape[2]
    assert Bp == 1

    nf = 11 * D_GRP
    w_main = jnp.concatenate([w_in[:, :, :nf], w_in[:, :, nf + H:]], axis=-1).astype(BF16)
    w_f = jnp.pad(w_in[:, :, nf:nf + H], ((0, 0), (0, 0), (0, LANES - H))).astype(BF16)
    bf_pad = jnp.pad(b_f, ((0, 0), (0, LANES - H))).reshape(L, 1, LANES)
    w_out_b = w_out.astype(BF16)

    nc = Bp + Bs
    c_all = jnp.pad(jnp.concatenate([c_prompt, c_sample], axis=0), ((0, (-nc) % 8), (0, 0)))
    mod = ada_mod(c_all, w_ada, b_ada)

    cache_kf = cache_k.reshape(L, Bs, P * H, DH)
    cache_vf = cache_v.reshape(L, Bs, P * H, DH)
    pad_rows = (-(L * Bs)) % 8
    hist_lf = jnp.pad(cache_logf.reshape(L * Bs, P * H), ((0, pad_rows), (0, 0)))
    dkh_all = cumsum_lanes(hist_lf, stride=H)[:L * Bs].reshape(L, Bs, 1, P * H)
    hist_total = dkh_all[:, :, :, (P - 1) * H:]

    xp, xs = x_prompt, x_sample
    zeros_a = jnp.zeros((Bp, A_CONV - 1, D_GRP), F32)
    zeros_b = jnp.zeros((Bp, B_CONV - 1, D_GRP), F32)
    outs_p = [[] for _ in range(6)]
    outs_s = [[] for _ in range(6)]
    for l in range(L):
        last = l == L - 1
        shift_p, scale_p, gate_p = (mod[l, :Bp, i * D:(i + 1) * D].reshape(Bp, 1, D) for i in range(3))
        shift_s, scale_s, gate_s = (mod[l, Bp:nc, i * D:(i + 1) * D].reshape(Bs, 1, D) for i in range(3))

        proj, k_p, v_p, lf_p = in_proj(xp, norm_w[l], scale_p, shift_p, w_main[l], w_f[l], bf_pad[l],
                                       bb=1, tt=IN_TM)
        ya, yb, na_p, nb_p = conv_ab(proj.reshape(Bp, Tp, -1), zeros_a, zeros_b, a_conv_w[l], b_conv_w[l],
                                     b_conv_b[l], b_ln_w[l], b_ln_b[l], tt=CONV_TT)
        lf_t = jnp.pad(lf_p[:, :H].T, ((0, 8 - H), (0, 0)))
        dcum = cumsum_lanes(lf_t)[:H]
        yc = fox_prompt(proj, dcum.reshape(H, Tp, 1), tile=ATT_T)
        yd, s_p = hgrn_prompt(proj, hgrn_lb_logits, d_norm_w[l], l)
        xp = out_proj(ya, yb, yc.reshape(Bp, Tp, D_GRP), yd.reshape(Bp, Tp, D_GRP), w_out_b[l], xp, gate_p,
                      final_norm_w, bb=1, tt=OUT_TM, final=last)
        for i, a in enumerate((na_p, nb_p, k_p.reshape(Bp, Tp, H, DH), v_p.reshape(Bp, Tp, H, DH),
                               lf_p[:, :H].reshape(Bp, Tp, H), s_p.reshape(Bp, H, DH, DH))):
            outs_p[i].append(a)

        proj, k_s, v_s, lf_s = in_proj(xs, norm_w[l], scale_s, shift_s, w_main[l], w_f[l], bf_pad[l],
                                       bb=Bs, tt=Ts)
        proj3 = proj.reshape(Bs, Ts, -1)
        ya, yb, na_s, nb_s = conv_ab(proj3, cache_a_conv[l], cache_b_conv[l], a_conv_w[l], b_conv_w[l],
                                     b_conv_b[l], b_ln_w[l], b_ln_b[l], tt=Ts)
        lf_new = lf_s[:, :H].reshape(Bs, Ts, H)
        lf_rows = jnp.pad(lf_new.reshape(Bs, Ts * H), ((0, (-Bs) % 8), (0, LANES - Ts * H)))
        dnew = cumsum_lanes(lf_rows, stride=H)[:Bs, :Ts * H].reshape(Bs, Ts, H) + hist_total[l]
        yc = fox_sample(proj3, cache_kf, cache_vf, l, dkh_all, dnew, jnp.transpose(dnew, (0, 2, 1)),
                        tk=ATT_TK_CACHE)
        yd, s_s = hgrn_sample(proj3, state_hgrn, hgrn_lb_logits, d_norm_w[l], l)
        xs = out_proj(ya, yb, yc, yd, w_out_b[l], xs, gate_s, final_norm_w, bb=Bs, tt=Ts, final=last)
        for i, a in enumerate((na_s, nb_s, k_s.reshape(Bs, Ts, H, DH), v_s.reshape(Bs, Ts, H, DH),
                               lf_new, s_s)):
            outs_s[i].append(a)

    res = [xp, xs]
    for i in range(6):
        res.append(jnp.stack(outs_p[i]))
        res.append(jnp.stack(outs_s[i]))
    return tuple(res)
```

```python
import functools

import jax
import jax.numpy as jnp
from jax import lax
from jax.experimental import pallas as pl
from jax.experimental.pallas import tpu as pltpu

F32 = jnp.float32
BF16 = jnp.bfloat16

DEPTH = 4
D_MODEL = 2048
D_GRP = 512
N_SPLIT = 15
H = 4
DH = 128
A_CONV = 3
B_CONV = 31
CHUNK = 64
EPS = 1e-6
LANES = 128
VMEM_LIMIT = 56 * 1024 * 1024

(S_AB, S_AC, S_AX, S_AZ, S_BA, S_BG, S_BZ, S_CQ, S_CK, S_CV, S_CZ, S_DQ, S_DF, S_DI, S_DZ) = range(15)

IN_TM = 1024
CONV_TT = 256
ATT_T = 512
ATT_TK_CACHE = 2048
OUT_TM = 512


def _cparams(*sem):
    return pltpu.CompilerParams(dimension_semantics=sem, vmem_limit_bytes=VMEM_LIMIT)


def _sigmoid(x):
    return 1.0 / (1.0 + jnp.exp(-x))


def _silu(x):
    return x * _sigmoid(x)


def _log_sigmoid(x):
    return jnp.minimum(x, 0.0) - jnp.log(1.0 + jnp.exp(-jnp.abs(x)))


def _split3(x):
    hi = x.astype(BF16)
    r1 = x - hi.astype(F32)
    mid = r1.astype(BF16)
    lo = (r1 - mid.astype(F32)).astype(BF16)
    return hi, mid, lo


def _dot(a, b):
    return jnp.dot(a, b, preferred_element_type=F32)


def _dot_nt(a, b):
    return lax.dot_general(a, b, (((1,), (1,)), ((), ())), preferred_element_type=F32)


def _ada_kernel(c_ref, w_ref, b_ref, o_ref):
    c = c_ref[...]
    a = _silu(c).astype(BF16)
    o_ref[...] = _dot(a, w_ref[...].astype(BF16)) + b_ref[...]


def ada_mod(c_all, w_ada, b_ada):
    L, D, N = w_ada.shape
    R = c_all.shape[0]
    tn = 768
    return pl.pallas_call(
        _ada_kernel,
        grid=(L, N // tn),
        in_specs=[pl.BlockSpec((R, D), lambda l, j: (0, 0)),
                  pl.BlockSpec((None, D, tn), lambda l, j: (l, 0, j)),
                  pl.BlockSpec((None, 1, tn), lambda l, j: (l, 0, j))],
        out_specs=pl.BlockSpec((None, R, tn), lambda l, j: (l, 0, j)),
        out_shape=jax.ShapeDtypeStruct((L, R, N), F32),
        compiler_params=_cparams("parallel", "parallel"),
        name="ada_mod",
    )(c_all, w_ada, b_ada.reshape(L, 1, N))


def _modulated_norm(x, nw, scale, shift):
    ms = jnp.mean(x * x, axis=-1, keepdims=True)
    h = (x * lax.rsqrt(ms + EPS) * nw) * (1.0 + scale) + shift
    return h.reshape(-1, h.shape[-1]).astype(BF16)


def _norm_mod_kernel(x_ref, nw_ref, sc_ref, sh_ref, h_ref):
    h_ref[...] = _modulated_norm(x_ref[...], nw_ref[...], sc_ref[...], sh_ref[...])


def norm_mod(x, norm_w, scale, shift, *, bb, tt):
    B, T, D = x.shape
    nt = T // tt
    return pl.pallas_call(
        _norm_mod_kernel,
        grid=((B // bb) * nt,),
        in_specs=[pl.BlockSpec((bb, tt, D), lambda i: (i // nt, i % nt, 0)),
                  pl.BlockSpec((1, 1, D), lambda i: (0, 0, 0)),
                  pl.BlockSpec((bb, 1, D), lambda i: (i // nt, 0, 0)),
                  pl.BlockSpec((bb, 1, D), lambda i: (i // nt, 0, 0))],
        out_specs=pl.BlockSpec((bb * tt, D), lambda i: (i, 0)),
        out_shape=jax.ShapeDtypeStruct((B * T, D), BF16),
        compiler_params=_cparams("parallel"),
        name="norm_mod",
    )(x, norm_w.reshape(1, 1, D), scale, shift)


N_F32_SPLITS = 11
IN_ORDER = (0, 1, 2, 3, 4, 5, 6, 7, 10, 11, 12, 13, 14, S_CK, S_CV)


def _split_of(j):
    return jnp.where(j < 8, j, jnp.where(j == 8, 10, jnp.where(j < 13, j + 2, j - 5)))


assert IN_ORDER == tuple(j if j < 8 else 10 if j == 8 else j + 2 if j < 13 else j - 5 for j in range(N_SPLIT))


def _inproj_kernel(h_ref, w_ref, wd_ref, wf_ref, bf_ref, proj_ref, kv_ref, logf_ref, wb_scr):
    j = pl.program_id(0)
    i = pl.program_id(1)
    split = _split_of(j)

    @pl.when((i == 0) & (split < N_F32_SPLITS))
    def _():
        wb_scr[...] = w_ref[...].astype(BF16)

    @pl.when((i == 0) & (split >= N_F32_SPLITS))
    def _():
        wb_scr[...] = wd_ref[...]

    h = h_ref[...]
    r = _dot(h, wb_scr[...])
    proj_ref[...] = r.astype(BF16)

    @pl.when(j >= N_SPLIT - 2)
    def _():
        kv_ref[...] = r

    @pl.when(j == 0)
    def _():
        logf_ref[...] = _log_sigmoid(_dot(h, wf_ref[...]) + bf_ref[...])


def in_proj(h, w_in, layer, w_d, w_f, b_f, *, tm):
    R, D = h.shape
    nrow = R // tm

    def w_map(j, i):
        return (layer, 0, jnp.minimum(_split_of(j), N_F32_SPLITS - 1))

    def wd_map(j, i):
        return (0, jnp.clip(_split_of(j) - N_F32_SPLITS, 0, N_SPLIT - N_F32_SPLITS - 1))

    def kv_map(j, i):
        last2 = j >= N_SPLIT - 2
        return (jnp.where(last2, i, 0), jnp.where(last2, j - (N_SPLIT - 2), 0))

    return pl.pallas_call(
        _inproj_kernel,
        grid=(N_SPLIT, nrow),
        in_specs=[pl.BlockSpec((tm, D), lambda j, i: (i, 0)),
                  pl.BlockSpec((None, D, D_GRP), w_map),
                  pl.BlockSpec((D, D_GRP), wd_map),
                  pl.BlockSpec((D, LANES), lambda j, i: (0, 0)),
                  pl.BlockSpec((1, LANES), lambda j, i: (0, 0))],
        out_specs=[pl.BlockSpec((tm, D_GRP), lambda j, i: (i, _split_of(j))),
                   pl.BlockSpec((tm, D_GRP), kv_map),
                   pl.BlockSpec((tm, LANES), lambda j, i: (jnp.where(j == 0, i, nrow - 1), 0))],
        out_shape=[jax.ShapeDtypeStruct((R, N_SPLIT * D_GRP), BF16),
                   jax.ShapeDtypeStruct((R, 2 * D_GRP), F32),
                   jax.ShapeDtypeStruct((R, LANES), F32)],
        scratch_shapes=[pltpu.VMEM((D, D_GRP), BF16)],
        compiler_params=_cparams("arbitrary", "arbitrary"),
        name="in_proj",
    )(h, w_in, w_d, w_f, b_f)


HA = 8
HB = 32

def _conv_kernel(ab_ref, ac_ref, ax_ref, az_ref, ba_ref, bg_ref, bz_ref, ha_ref, hb_ref,
                 wa_ref, wb_ref, bb_ref, lw_ref, lb_ref,
                 ya_ref, yb_ref, na_ref, nb_ref, seqa, seqb, *, tt):
    t = pl.program_id(1)
    na, nb = A_CONV - 1, B_CONV - 1

    @pl.when(t == 0)
    def _():
        seqa[HA - na:HA, :] = ha_ref[0]
        seqb[0, 0:HB - nb, :] = jnp.zeros((HB - nb, D_GRP), F32)
        seqb[0, HB - nb:HB, :] = hb_ref[0]

    @pl.when(t > 0)
    def _():
        ta = seqa[HA + tt - na:HA + tt, :]
        seqa[HA - na:HA, :] = ta
        tb = seqb[0, HB + tt - nb:HB + tt, :]
        seqb[0, HB - nb:HB, :] = tb

    seqa[HA:HA + tt, :] = ac_ref[0].astype(F32) * ax_ref[0].astype(F32)
    seqb[0, HB:HB + tt, :] = ba_ref[0].astype(F32) * _sigmoid(bg_ref[0].astype(F32))
    for r in range(1, 8):
        seqb[r, 0:HB + tt - r, :] = seqb[0, r:HB + tt, :]

    rc = min(tt, 32)
    for r0 in range(0, tt, rc):
        acc = wa_ref[0:1, :] * seqa[HA - na + r0:HA - na + r0 + rc, :]
        for w in range(1, A_CONV):
            acc = acc + wa_ref[w:w + 1, :] * seqa[HA - na + r0 + w:HA - na + r0 + w + rc, :]
        ya = _silu(az_ref[0, r0:r0 + rc, :].astype(F32)) * ab_ref[0, r0:r0 + rc, :].astype(F32) * acc
        ya_ref[0, r0:r0 + rc, :] = ya.astype(BF16)

        acc = None
        for w in range(B_CONV):
            a8, r8 = divmod(HB - nb + w, 8)
            term = wb_ref[w:w + 1, :] * seqb[r8, 8 * a8 + r0:8 * a8 + r0 + rc, :]
            acc = term if acc is None else acc + term
        y = acc + bb_ref[...]
        mu = jnp.mean(y, axis=-1, keepdims=True)
        yc = y - mu
        var = jnp.mean(yc * yc, axis=-1, keepdims=True)
        yn = yc * lax.rsqrt(var + EPS) * lw_ref[...] + lb_ref[...]
        yb = _silu(bz_ref[0, r0:r0 + rc, :].astype(F32)) * _silu(yn)
        yb_ref[0, r0:r0 + rc, :] = yb.astype(BF16)

    na_ref[0] = seqa[HA + tt - na:HA + tt, :]
    nb_ref[0] = seqb[0, HB + tt - nb:HB + tt, :]


def conv_ab(proj3, hist_a, hist_b, wa, wb, bias_b, ln_w, ln_b, *, tt):
    B, T, _ = proj3.shape
    nt = T // tt

    def col(g):
        return pl.BlockSpec((1, tt, D_GRP), lambda b, t, g=g: (b, t, g))

    def full(a):
        return pl.BlockSpec(a.shape, lambda b, t: (0,) * a.ndim)

    bias_b, ln_w, ln_b = (a.reshape(1, D_GRP) for a in (bias_b, ln_w, ln_b))
    return pl.pallas_call(
        functools.partial(_conv_kernel, tt=tt),
        grid=(B, nt),
        in_specs=[col(S_AB), col(S_AC), col(S_AX), col(S_AZ), col(S_BA), col(S_BG), col(S_BZ),
                  pl.BlockSpec((1, A_CONV - 1, D_GRP), lambda b, t: (b, 0, 0)),
                  pl.BlockSpec((1, B_CONV - 1, D_GRP), lambda b, t: (b, 0, 0)),
                  full(wa), full(wb), full(bias_b), full(ln_w), full(ln_b)],
        out_specs=[pl.BlockSpec((1, tt, D_GRP), lambda b, t: (b, t, 0)),
                   pl.BlockSpec((1, tt, D_GRP), lambda b, t: (b, t, 0)),
                   pl.BlockSpec((1, A_CONV - 1, D_GRP), lambda b, t: (b, 0, 0)),
                   pl.BlockSpec((1, B_CONV - 1, D_GRP), lambda b, t: (b, 0, 0))],
        out_shape=[jax.ShapeDtypeStruct((B, T, D_GRP), BF16),
                   jax.ShapeDtypeStruct((B, T, D_GRP), BF16),
                   jax.ShapeDtypeStruct((B, A_CONV - 1, D_GRP), F32),
                   jax.ShapeDtypeStruct((B, B_CONV - 1, D_GRP), F32)],
        scratch_shapes=[pltpu.VMEM((HA + tt, D_GRP), F32), pltpu.VMEM((8, HB + tt, D_GRP), F32)],
        compiler_params=_cparams("parallel", "arbitrary"),
        name="conv_ab",
    )(proj3, proj3, proj3, proj3, proj3, proj3, proj3, hist_a, hist_b, wa, wb, bias_b, ln_w, ln_b)


def _cumsum_kernel(x_ref, o_ref, carry, *, cb, stride):
    j = pl.program_id(0)

    @pl.when(j == 0)
    def _():
        carry[...] = jnp.zeros_like(carry)

    lane = lax.broadcasted_iota(jnp.int32, carry.shape, 1)
    head = jnp.where(lane < stride, pltpu.roll(carry[...], stride, 1), 0.0)
    x = x_ref[...]
    x = jnp.concatenate([x[:, 0:LANES] + head, x[:, LANES:]], axis=1) if cb > LANES else x + head
    hi, mid, lo = _split3(x)
    r = lax.broadcasted_iota(jnp.int32, (cb, cb), 0)
    c = lax.broadcasted_iota(jnp.int32, (cb, cb), 1)
    same_series = lax.bitwise_and(r, stride - 1) == lax.bitwise_and(c, stride - 1)
    u = ((r <= c) & same_series).astype(BF16)
    cs = _dot(hi, u) + _dot(mid, u) + _dot(lo, u)
    o_ref[...] = cs
    carry[...] = cs[:, cb - LANES:cb]


def cumsum_lanes(x, stride=1):
    R, L = x.shape
    cb = 256 if L % 256 == 0 else LANES
    return pl.pallas_call(
        functools.partial(_cumsum_kernel, cb=cb, stride=stride),
        grid=(L // cb,),
        in_specs=[pl.BlockSpec((R, cb), lambda j: (0, j))],
        out_specs=pl.BlockSpec((R, cb), lambda j: (0, j)),
        out_shape=jax.ShapeDtypeStruct((R, L), F32),
        scratch_shapes=[pltpu.VMEM((R, LANES), F32)],
        compiler_params=_cparams("arbitrary"),
        name="cumsum_lanes",
    )(x)


LOG2E = 1.4426950408889634
ATT_PREP_ROWS = 1024
ATT_SPLIT = 2


def _bias_lanes(d, key_side):
    hi, mid, lo = (p.astype(F32) for p in _split3(d))
    lane = lax.broadcasted_iota(jnp.int32, (d.shape[0], LANES), 1)
    if key_side:
        a = jnp.where(lane == 3, -hi, jnp.where(lane == 4, -mid, jnp.where(lane == 5, -lo, 0.0)))
        a = jnp.where(lane < 3, 1.0, a)
    else:
        a = jnp.where(lane == 0, hi, jnp.where(lane == 1, mid, jnp.where(lane == 2, lo, 0.0)))
        a = jnp.where((lane >= 3) & (lane < 6), 1.0, a)
    return a.astype(BF16)


def _fox_prompt_kernel(q_ref, k_ref, v_ref, z_ref, dq_ref, dk_ref, y_ref,
                       ka_scr, va_scr, qa_scr, s0_scr, s1_scr, p_scr, alpha_scr, m_scr, l_scr, acc_scr, *, tile):
    qi = pl.program_id(1)
    T = k_ref.shape[0]
    th = tile // ATT_SPLIT

    @pl.when(qi == 0)
    def _():
        pr = min(T, ATT_PREP_ROWS)
        one_lane = (lax.broadcasted_iota(jnp.int32, (pr, LANES), 1) == 0).astype(BF16)

        def prep(i, _):
            rows = pl.ds(pl.multiple_of(i * pr, pr), pr)
            ka_scr[rows, 0:DH] = k_ref[rows, :]
            ka_scr[rows, DH:2 * DH] = _bias_lanes(dk_ref[rows, :] * LOG2E, True)
            va_scr[rows, 0:DH] = v_ref[rows, :]
            va_scr[rows, DH:2 * DH] = one_lane
            return 0

        lax.fori_loop(0, T // pr, prep, 0)

    qa_scr[:, 0:DH] = (q_ref[...].astype(F32) * (DH ** -0.5 * LOG2E)).astype(BF16)
    qa_scr[:, DH:2 * DH] = _bias_lanes(dq_ref[...] * LOG2E, False)

    m_scr[...] = jnp.full(m_scr.shape, -jnp.inf, F32)
    l_scr[...] = jnp.zeros(l_scr.shape, F32)
    acc_scr[...] = jnp.zeros(acc_scr.shape, F32)

    def logits(ki, s_buf):
        off = pl.multiple_of(ki * tile, tile)
        ka = ka_scr[pl.ds(off, tile), :]
        for g in range(ATT_SPLIT):
            rows = slice(g * th, (g + 1) * th)
            s_buf[rows, :] = _dot_nt(qa_scr[rows, :], ka)

    def softmax(s_buf, masked):
        for g in range(ATT_SPLIT):
            rows = slice(g * th, (g + 1) * th)
            s = s_buf[rows, :]
            if masked:
                r = lax.broadcasted_iota(jnp.int32, (th, tile), 0) + g * th
                c = lax.broadcasted_iota(jnp.int32, (th, tile), 1)
                s = jnp.where(c <= r, s, -jnp.inf)
            m = m_scr[rows, :]
            m_new = jnp.maximum(m, jnp.max(s, axis=-1, keepdims=True))
            alpha_scr[rows, :] = jnp.exp2(m - m_new)
            p_scr[rows, :] = jnp.exp2(s - m_new).astype(BF16)
            m_scr[rows, :] = m_new

    def weighted_values(ki):
        off = pl.multiple_of(ki * tile, tile)
        va = va_scr[pl.ds(off, tile), :]
        for g in range(ATT_SPLIT):
            rows = slice(g * th, (g + 1) * th)
            pv = _dot(p_scr[rows, :], va)
            alpha = alpha_scr[rows, :]
            l_scr[rows, :] = alpha * l_scr[rows, :] + pv[:, DH:DH + 1]
            acc_scr[rows, :] = alpha * acc_scr[rows, :] + pv[:, 0:DH]

    p_scr[...] = jnp.zeros(p_scr.shape, BF16)
    alpha_scr[...] = jnp.ones(alpha_scr.shape, F32)
    s_bufs = (s0_scr, s1_scr)
    logits(0, s_bufs[0])

    def stage(ki, parity, masked):
        weighted_values(jnp.maximum(ki - 1, 0))
        softmax(s_bufs[parity], masked)
        if not masked:
            logits(ki + 1, s_bufs[1 - parity])

    def pair(j, _):
        stage(2 * j, 0, False)
        stage(2 * j + 1, 1, False)
        return 0

    lax.fori_loop(0, lax.shift_right_logical(qi, 1), pair, 0)
    odd = lax.bitwise_and(qi, 1) == 1

    @pl.when(odd)
    def _():
        stage(qi - 1, 0, False)
        stage(qi, 1, True)

    @pl.when(jnp.logical_not(odd))
    def _():
        stage(qi, 0, True)

    weighted_values(qi)
    y_ref[...] = (_silu(z_ref[...].astype(F32)) * (acc_scr[...] / l_scr[...])).astype(BF16)


def fox_prompt(proj, dcol, *, tile):
    T = proj.shape[0]
    cpb = D_GRP // DH
    return pl.pallas_call(
        functools.partial(_fox_prompt_kernel, tile=tile),
        grid=(H, T // tile),
        in_specs=[pl.BlockSpec((tile, DH), lambda h, i: (i, S_CQ * cpb + h)),
                  pl.BlockSpec((T, DH), lambda h, i: (0, S_CK * cpb + h)),
                  pl.BlockSpec((T, DH), lambda h, i: (0, S_CV * cpb + h)),
                  pl.BlockSpec((tile, DH), lambda h, i: (i, S_CZ * cpb + h)),
                  pl.BlockSpec((None, tile, 1), lambda h, i: (h, i, 0)),
                  pl.BlockSpec((None, T, 1), lambda h, i: (h, 0, 0))],
        out_specs=pl.BlockSpec((tile, DH), lambda h, i: (i, h)),
        out_shape=jax.ShapeDtypeStruct((T, D_GRP), BF16),
        scratch_shapes=[pltpu.VMEM((T, 2 * DH), BF16), pltpu.VMEM((T, 2 * DH), BF16),
                        pltpu.VMEM((tile, 2 * DH), BF16), pltpu.VMEM((tile, tile), F32),
                        pltpu.VMEM((tile, tile), F32), pltpu.VMEM((tile, tile), BF16), pltpu.VMEM((tile, 1), F32),
                        pltpu.VMEM((tile, 1), F32), pltpu.VMEM((tile, 1), F32), pltpu.VMEM((tile, DH), F32)],
        compiler_params=_cparams("parallel", "arbitrary"),
        name="fox_prompt",
    )(proj, proj, proj, proj, dcol, dcol)


def _fox_sample_kernel(q_ref, kn_ref, vn_ref, z_ref, kc_ref, vc_ref, dkh_ref, dq_ref, dkn_ref,
                       y_ref, m_scr, l_scr, acc_scr, *, tq):
    kt = pl.program_id(1)
    nk = pl.num_programs(1)
    scale = DH ** -0.5

    @pl.when(kt == 0)
    def _():
        m_scr[...] = jnp.full(m_scr.shape, -jnp.inf, F32)
        l_scr[...] = jnp.zeros(l_scr.shape, F32)
        acc_scr[...] = jnp.zeros(acc_scr.shape, F32)

    def update(rows, s, v):
        m = m_scr[rows, :]
        m_new = jnp.maximum(m, jnp.max(s, axis=-1, keepdims=True))
        alpha = jnp.exp(m - m_new)
        p = jnp.exp(s - m_new)
        l_scr[rows, :] = alpha * l_scr[rows, :] + jnp.sum(p, axis=-1, keepdims=True)
        acc_scr[rows, :] = alpha * acc_scr[rows, :] + _dot(p.astype(BF16), v)
        m_scr[rows, :] = m_new

    heads = [slice(h * DH, (h + 1) * DH) for h in range(H)]
    q_stack = jnp.concatenate([q_ref[0, :, cs] for cs in heads], axis=0)
    dq_stack = jnp.concatenate([dq_ref[0, :, h:h + 1] for h in range(H)], axis=0)
    kf = kc_ref[0].astype(BF16)
    n = kf.shape[0]
    s = _dot_nt(q_stack, kf) * scale + (dq_stack - dkh_ref[0])
    row_head = lax.shift_right_logical(lax.broadcasted_iota(jnp.int32, (H * tq, n), 0), tq.bit_length() - 1)
    col_head = lax.bitwise_and(lax.broadcasted_iota(jnp.int32, (H * tq, n), 1), H - 1)
    update(slice(None), jnp.where(row_head == col_head, s, -jnp.inf), vc_ref[0].astype(BF16))

    @pl.when(kt == nk - 1)
    def _():
        r = lax.broadcasted_iota(jnp.int32, (tq, tq), 0)
        c = lax.broadcasted_iota(jnp.int32, (tq, tq), 1)
        for h in range(H):
            rows = slice(h * tq, (h + 1) * tq)
            s_new = _dot_nt(q_ref[0, :, heads[h]], kn_ref[0, :, heads[h]]) * scale \
                + (dq_ref[0, :, h:h + 1] - dkn_ref[0, h:h + 1, :])
            update(rows, jnp.where(c <= r, s_new, -jnp.inf), vn_ref[0, :, heads[h]])
            o = acc_scr[rows, :] / l_scr[rows, :]
            y_ref[0, :, heads[h]] = (_silu(z_ref[0, :, heads[h]].astype(F32)) * o).astype(BF16)


def fox_sample(proj3, cache_k, cache_v, layer, dkh, dq_col, dkn_row, *, tk):
    B, tq, _ = proj3.shape
    n = tk * H
    assert tq & (tq - 1) == 0 and H & (H - 1) == 0

    def col(g):
        return pl.BlockSpec((1, tq, D_GRP), lambda b, t, g=g: (b, 0, g))

    cache_spec = pl.BlockSpec((None, 1, n, DH), lambda b, t: (layer, b, t, 0))
    return pl.pallas_call(
        functools.partial(_fox_sample_kernel, tq=tq),
        grid=(B, cache_k.shape[2] // n),
        in_specs=[col(S_CQ), col(S_CK), col(S_CV), col(S_CZ), cache_spec, cache_spec,
                  pl.BlockSpec((None, 1, 1, n), lambda b, t: (layer, b, 0, t)),
                  pl.BlockSpec((1, tq, H), lambda b, t: (b, 0, 0)),
                  pl.BlockSpec((1, H, tq), lambda b, t: (b, 0, 0))],
        out_specs=pl.BlockSpec((1, tq, D_GRP), lambda b, t: (b, 0, 0)),
        out_shape=jax.ShapeDtypeStruct((B, tq, D_GRP), BF16),
        scratch_shapes=[pltpu.VMEM((H * tq, 1), F32), pltpu.VMEM((H * tq, 1), F32),
                        pltpu.VMEM((H * tq, DH), F32)],
        compiler_params=_cparams("parallel", "arbitrary"),
        name="fox_sample",
    )(proj3, proj3, proj3, proj3, cache_k, cache_v, dkh, dq_col, dkn_row)


def _lower_bound(lbl_ref, layer):
    x = lbl_ref[...]
    e = jnp.exp(x - jnp.max(x, axis=0, keepdims=True))
    sm = e / jnp.sum(e, axis=0, keepdims=True)
    cum = sm[0:1, :]
    first = cum
    for i in range(1, layer + 1):
        cum = cum + sm[i:i + 1, :]
    return cum - first


def _gla_levels(C):
    return [C >> i for i in range(1, C.bit_length())]


def _gla_group(q, gl, vi, lb, st_list, b_scr, *, C):
    G = len(st_list)
    N = G * C
    sig = _sigmoid(gl)
    g = jnp.log(lb + (1.0 - lb) * sig)
    kk = (1.0 - lb) * (1.0 - sig)

    row = lax.broadcasted_iota(jnp.int32, (N, N), 0)
    colm = lax.broadcasted_iota(jnp.int32, (N, N), 1)

    def same_block(size):
        sh = size.bit_length() - 1
        return lax.shift_right_logical(row, sh) == lax.shift_right_logical(colm, sh)

    tril = (same_block(C) & (colm <= row)).astype(BF16)
    g_hi, g_mid, g_lo = _split3(g)
    b = _dot(tril, g_hi) + _dot(tril, g_mid) + _dot(tril, g_lo)
    b_scr[...] = b

    def gather_rows(size, offset):
        parts = [jnp.broadcast_to(b_scr[i * size + offset:i * size + offset + 1, :], (size, LANES))
                 for i in range(N // size)]
        return parts[0] if len(parts) == 1 else jnp.concatenate(parts, axis=0)

    trow = lax.broadcasted_iota(jnp.int32, (N, LANES), 0)
    qb16 = q.astype(BF16)
    kb16 = kk.astype(BF16)
    a = jnp.where(row == colm, _dot_nt(qb16, kb16), 0.0)
    for m in _gla_levels(C):
        size = 2 * m
        u = lax.bitwise_and(trow, size - 1)
        upper = u >= m
        if size >= 8:
            ref = gather_rows(size, m - 1)
        elif size == 4:
            ref = jnp.where(u == 0, pltpu.roll(b, N - 1, 0),
                            jnp.where(u == 1, b, jnp.where(u == 2, pltpu.roll(b, 1, 0), pltpu.roll(b, 2, 0))))
        else:
            ref = jnp.where(u == 1, pltpu.roll(b, 1, 0), b)
        x = jnp.exp(jnp.where(upper, b - ref, ref - b))
        qm = jnp.where(upper, q * x, 0.0).astype(BF16)
        km = jnp.where(upper, 0.0, kk * x).astype(BF16)
        a = a + jnp.where(same_block(size), _dot_nt(qm, km), 0.0)

    vb16 = vi.astype(BF16)
    o = _dot(a.astype(BF16), vb16)

    qin = (q * jnp.exp(b)).astype(BF16)
    kd = kk * jnp.exp(gather_rows(C, C - 1) - b)
    vt = vi.T.astype(BF16)
    o_parts, st_new = [], []
    for gi in range(G):
        st = st_list[gi]
        o_parts.append(_dot_nt(qin[gi * C:(gi + 1) * C, :], st.astype(BF16)))
        seg = lax.shift_right_logical(trow, C.bit_length() - 1) == gi
        kd_g = jnp.where(seg, kd, 0.0).astype(BF16)
        ebl = jnp.exp(b_scr[(gi + 1) * C - 1:(gi + 1) * C, :])
        st_new.append(st * ebl + _dot(vt, kd_g))
    o = o + (o_parts[0] if G == 1 else jnp.concatenate(o_parts, axis=0))
    return o, st_new


def _gla_out(o, z, dn):
    ms = jnp.mean(o * o, axis=-1, keepdims=True)
    return (_silu(z) * (o * lax.rsqrt(ms + EPS) * dn)).astype(BF16)


def _hgrn_prompt_kernel(q_ref, f_ref, i_ref, z_ref, lbl_ref, dn_ref, y_ref, s_ref, st_scr, b_scr,
                        *, layer, nseq):
    step = pl.program_id(0)
    C = CHUNK

    @pl.when(step == 0)
    def _():
        st_scr[...] = jnp.zeros(st_scr.shape, F32)

    lb_row = _lower_bound(lbl_ref, layer)
    lb = jnp.concatenate([jnp.broadcast_to(lb_row[:, h * DH:(h + 1) * DH], (C, DH)) for h in range(H)], axis=0)
    dn = dn_ref[...]

    def stack(ref, r0):
        return jnp.concatenate([ref[r0:r0 + C, h * DH:(h + 1) * DH].astype(F32) for h in range(H)], axis=0)

    for cidx in range(nseq):
        r0 = cidx * C
        st_list = [st_scr[h] for h in range(H)]
        o, st_new = _gla_group(stack(q_ref, r0), stack(f_ref, r0), stack(i_ref, r0), lb, st_list, b_scr, C=C)
        y = _gla_out(o, stack(z_ref, r0), dn)
        for h in range(H):
            st_scr[h] = st_new[h]
            y_ref[r0:r0 + C, h * DH:(h + 1) * DH] = y[h * C:(h + 1) * C, :]

    @pl.when(step == pl.num_programs(0) - 1)
    def _():
        for h in range(H):
            s_ref[h] = st_scr[h].T


def hgrn_prompt(proj, lb_logits, d_norm, layer, *, nseq=4):
    T = proj.shape[0]
    rows = nseq * CHUNK

    def col(g):
        return pl.BlockSpec((rows, D_GRP), lambda i, g=g: (i, g))

    return pl.pallas_call(
        functools.partial(_hgrn_prompt_kernel, layer=layer, nseq=nseq),
        grid=(T // rows,),
        in_specs=[col(S_DQ), col(S_DF), col(S_DI), col(S_DZ),
                  pl.BlockSpec(lb_logits.shape, lambda i: (0, 0)),
                  pl.BlockSpec((1, DH), lambda i: (0, 0))],
        out_specs=[pl.BlockSpec((rows, D_GRP), lambda i: (i, 0)),
                   pl.BlockSpec((H, DH, DH), lambda i: (0, 0, 0))],
        out_shape=[jax.ShapeDtypeStruct((T, D_GRP), BF16),
                   jax.ShapeDtypeStruct((H, DH, DH), F32)],
        scratch_shapes=[pltpu.VMEM((H, DH, DH), F32), pltpu.VMEM((H * CHUNK, DH), F32)],
        compiler_params=_cparams("arbitrary"),
        name="hgrn_prompt",
    )(proj, proj, proj, proj, lb_logits, d_norm.reshape(1, DH))


def _hgrn_sample_kernel(q_ref, f_ref, i_ref, z_ref, s0_ref, lbl_ref, dn_ref, y_ref, s_ref, b_scr,
                        *, layer, nb, C):
    lb_row = _lower_bound(lbl_ref, layer)
    segs = [(bi, h) for bi in range(nb) for h in range(H)]
    lb = jnp.concatenate([jnp.broadcast_to(lb_row[:, h * DH:(h + 1) * DH], (C, DH)) for _, h in segs], axis=0)

    def stack(ref):
        return jnp.concatenate([ref[bi, :, h * DH:(h + 1) * DH].astype(F32) for bi, h in segs], axis=0)

    st_list = [s0_ref[bi, h].T for bi, h in segs]
    o, st_new = _gla_group(stack(q_ref), stack(f_ref), stack(i_ref), lb, st_list, b_scr, C=C)
    y = _gla_out(o, stack(z_ref), dn_ref[...])
    for gi, (bi, h) in enumerate(segs):
        s_ref[bi, h] = st_new[gi].T
        y_ref[bi, :, h * DH:(h + 1) * DH] = y[gi * C:(gi + 1) * C, :]


def hgrn_sample(proj3, state, lb_logits, d_norm, layer, *, nb=4):
    B, C, _ = proj3.shape

    def col(g):
        return pl.BlockSpec((nb, C, D_GRP), lambda i, g=g: (i, 0, g))

    return pl.pallas_call(
        functools.partial(_hgrn_sample_kernel, layer=layer, nb=nb, C=C),
        grid=(B // nb,),
        in_specs=[col(S_DQ), col(S_DF), col(S_DI), col(S_DZ),
                  pl.BlockSpec((None, nb, H, DH, DH), lambda i: (layer, i, 0, 0, 0)),
                  pl.BlockSpec(lb_logits.shape, lambda i: (0, 0)),
                  pl.BlockSpec((1, DH), lambda i: (0, 0))],
        out_specs=[pl.BlockSpec((nb, C, D_GRP), lambda i: (i, 0, 0)),
                   pl.BlockSpec((nb, H, DH, DH), lambda i: (i, 0, 0, 0))],
        out_shape=[jax.ShapeDtypeStruct((B, C, D_GRP), BF16),
                   jax.ShapeDtypeStruct((B, H, DH, DH), F32)],
        scratch_shapes=[pltpu.VMEM((nb * H * C, DH), F32)],
        compiler_params=_cparams("parallel"),
        name="hgrn_sample",
    )(proj3, proj3, proj3, proj3, state, lb_logits, d_norm.reshape(1, DH))


def _outproj_kernel(ya_ref, yb_ref, yc_ref, yd_ref, w_ref, x_ref, g_ref, nw_ref, sc_ref, sh_ref, *out_refs, final):
    acc = _dot(ya_ref[...].reshape(-1, D_GRP), w_ref[0:D_GRP, :])
    for i, ref in enumerate((yb_ref, yc_ref, yd_ref), start=1):
        acc = acc + _dot(ref[...].reshape(-1, D_GRP), w_ref[i * D_GRP:(i + 1) * D_GRP, :])
    x = x_ref[...]
    xn = x + g_ref[...] * acc.reshape(x.shape)
    if final:
        ms = jnp.mean(xn * xn, axis=-1, keepdims=True)
        out_refs[0][...] = xn * lax.rsqrt(ms + EPS) * nw_ref[...]
    else:
        out_refs[0][...] = xn
        out_refs[1][...] = _modulated_norm(xn, nw_ref[...], sc_ref[...], sh_ref[...])


def out_proj(ya, yb, yc, yd, w_out, x, gate, norm_w, scale, shift, *, bb, tt, final):
    B, T, D = x.shape
    nt = T // tt

    def ymap(i):
        return (i // nt, i % nt, 0)

    def bmap(i):
        return (i // nt, 0, 0)

    yspec = pl.BlockSpec((bb, tt, D_GRP), ymap)
    out_specs = [pl.BlockSpec((bb, tt, D), ymap)]
    out_shape = [jax.ShapeDtypeStruct((B, T, D), F32)]
    if not final:
        out_specs.append(pl.BlockSpec((bb * tt, D), lambda i: (i, 0)))
        out_shape.append(jax.ShapeDtypeStruct((B * T, D), BF16))
    return pl.pallas_call(
        functools.partial(_outproj_kernel, final=final),
        grid=((B // bb) * nt,),
        in_specs=[yspec, yspec, yspec, yspec,
                  pl.BlockSpec(w_out.shape, lambda i: (0, 0)),
                  pl.BlockSpec((bb, tt, D), ymap),
                  pl.BlockSpec((bb, 1, D), bmap),
                  pl.BlockSpec((1, 1, D), lambda i: (0, 0, 0)),
                  pl.BlockSpec((bb, 1, D), bmap),
                  pl.BlockSpec((bb, 1, D), bmap)],
        out_specs=out_specs,
        out_shape=out_shape,
        compiler_params=_cparams("parallel"),
        name="out_proj",
    )(ya, yb, yc, yd, w_out, x, gate, norm_w.reshape(1, 1, D), scale, shift)


def kernel(x_prompt, x_sample, c_prompt, c_sample, cache_a_conv, cache_b_conv, cache_k, cache_v, cache_logf, state_hgrn, norm_w, w_ada, b_ada, w_in, b_f, a_conv_w, b_conv_w, b_conv_b, b_ln_w, b_ln_b, d_norm_w, hgrn_lb_logits, w_out, final_norm_w):
    L = DEPTH
    Bp, Tp, D = x_prompt.shape
    Bs, Ts, _ = x_sample.shape
    P = cache_k.shape[2]
    assert Bp == 1

    nf = N_F32_SPLITS * D_GRP
    w_d = w_in[:, :, nf + H:].astype(BF16)
    w_f = jnp.pad(w_in[:, :, nf:nf + H], ((0, 0), (0, 0), (0, LANES - H))).astype(BF16)
    bf_pad = jnp.pad(b_f, ((0, 0), (0, LANES - H))).reshape(L, 1, LANES)
    w_out_b = w_out.astype(BF16)

    nc = Bp + Bs
    c_all = jnp.pad(jnp.concatenate([c_prompt, c_sample], axis=0), ((0, (-nc) % 8), (0, 0)))
    mod = ada_mod(c_all, w_ada, b_ada)

    cache_kf = cache_k.reshape(L, Bs, P * H, DH)
    cache_vf = cache_v.reshape(L, Bs, P * H, DH)
    pad_rows = (-(L * Bs)) % 8
    hist_lf = jnp.pad(cache_logf.reshape(L * Bs, P * H), ((0, pad_rows), (0, 0)))
    dkh_all = cumsum_lanes(hist_lf, stride=H)[:L * Bs].reshape(L, Bs, 1, P * H)
    hist_total = dkh_all[:, :, :, (P - 1) * H:]

    xp, xs = x_prompt, x_sample
    zeros_a = jnp.zeros((Bp, A_CONV - 1, D_GRP), F32)
    zeros_b = jnp.zeros((Bp, B_CONV - 1, D_GRP), F32)
    outs_p = [[] for _ in range(6)]
    outs_s = [[] for _ in range(6)]

    def mods(l, lo, hi):
        return tuple(mod[l, lo:hi, i * D:(i + 1) * D].reshape(hi - lo, 1, D) for i in range(3))

    shift_p, scale_p, gate_p = mods(0, 0, Bp)
    shift_s, scale_s, gate_s = mods(0, Bp, nc)
    hp = norm_mod(xp, norm_w[0], scale_p, shift_p, bb=1, tt=OUT_TM)
    hs = norm_mod(xs, norm_w[0], scale_s, shift_s, bb=Bs, tt=Ts)
    for l in range(L):
        last = l == L - 1
        if last:
            next_w, next_p, next_s = final_norm_w, (gate_p, gate_p), (gate_s, gate_s)
        else:
            sh_p, sc_p, g_p = mods(l + 1, 0, Bp)
            sh_s, sc_s, g_s = mods(l + 1, Bp, nc)
            next_w, next_p, next_s = norm_w[l + 1], (sc_p, sh_p), (sc_s, sh_s)

        proj, kv_p, lf_p = in_proj(hp, w_in, l, w_d[l], w_f[l], bf_pad[l], tm=min(IN_TM, Tp))
        k_p, v_p = kv_p[:, :D_GRP], kv_p[:, D_GRP:]
        ya, yb, na_p, nb_p = conv_ab(proj.reshape(Bp, Tp, -1), zeros_a, zeros_b, a_conv_w[l], b_conv_w[l],
                                     b_conv_b[l], b_ln_w[l], b_ln_b[l], tt=CONV_TT)
        lf_t = jnp.pad(lf_p[:, :H].T, ((0, 8 - H), (0, 0)))
        dcum = cumsum_lanes(lf_t)[:H]
        yc = fox_prompt(proj, dcum.reshape(H, Tp, 1), tile=ATT_T)
        yd, s_p = hgrn_prompt(proj, hgrn_lb_logits, d_norm_w[l], l)
        res_p = out_proj(ya, yb, yc.reshape(Bp, Tp, D_GRP), yd.reshape(Bp, Tp, D_GRP), w_out_b[l], xp, gate_p,
                         next_w, *next_p, bb=1, tt=OUT_TM, final=last)
        xp, hp = res_p if not last else (res_p[0], None)
        for i, a in enumerate((na_p, nb_p, k_p.reshape(Bp, Tp, H, DH), v_p.reshape(Bp, Tp, H, DH),
                               lf_p[:, :H].reshape(Bp, Tp, H), s_p.reshape(Bp, H, DH, DH))):
            outs_p[i].append(a)

        proj, kv_s, lf_s = in_proj(hs, w_in, l, w_d[l], w_f[l], bf_pad[l], tm=Bs * Ts)
        k_s, v_s = kv_s[:, :D_GRP], kv_s[:, D_GRP:]
        proj3 = proj.reshape(Bs, Ts, -1)
        ya, yb, na_s, nb_s = conv_ab(proj3, cache_a_conv[l], cache_b_conv[l], a_conv_w[l], b_conv_w[l],
                                     b_conv_b[l], b_ln_w[l], b_ln_b[l], tt=Ts)
        lf_new = lf_s[:, :H].reshape(Bs, Ts, H)
        lf_rows = jnp.pad(lf_new.reshape(Bs, Ts * H), ((0, (-Bs) % 8), (0, LANES - Ts * H)))
        dnew = cumsum_lanes(lf_rows, stride=H)[:Bs, :Ts * H].reshape(Bs, Ts, H) + hist_total[l]
        yc = fox_sample(proj3, cache_kf, cache_vf, l, dkh_all, dnew, jnp.transpose(dnew, (0, 2, 1)),
                        tk=min(ATT_TK_CACHE, P))
        yd, s_s = hgrn_sample(proj3, state_hgrn, hgrn_lb_logits, d_norm_w[l], l)
        res_s = out_proj(ya, yb, yc, yd, w_out_b[l], xs, gate_s, next_w, *next_s, bb=Bs, tt=Ts, final=last)
        xs, hs = res_s if not last else (res_s[0], None)
        if not last:
            gate_p, gate_s = g_p, g_s
        for i, a in enumerate((na_s, nb_s, k_s.reshape(Bs, Ts, H, DH), v_s.reshape(Bs, Ts, H, DH),
                               lf_new, s_s)):
            outs_s[i].append(a)

    res = [xp, xs]
    for i in range(6):
        res.append(jnp.stack(outs_p[i]))
        res.append(jnp.stack(outs_s[i]))
    return tuple(res)
```

```python
import functools

import jax
import jax.numpy as jnp
from jax import lax
from jax.experimental import pallas as pl
from jax.experimental.pallas import tpu as pltpu

F32 = jnp.float32
BF16 = jnp.bfloat16

DEPTH = 4
D_MODEL = 2048
D_GRP = 512
N_SPLIT = 15
H = 4
DH = 128
A_CONV = 3
B_CONV = 31
CHUNK = 64
EPS = 1e-6
LANES = 128
VMEM_LIMIT = 56 * 1024 * 1024

(S_AB, S_AC, S_AX, S_AZ, S_BA, S_BG, S_BZ, S_CQ, S_CK, S_CV, S_CZ, S_DQ, S_DF, S_DI, S_DZ) = range(15)

IN_TN = 1536
IN_TM = 1024
CONV_TT = 256
ATT_T = 512
ATT_TK_CACHE = 4096
OUT_TM = 512


def _cparams(*sem):
    return pltpu.CompilerParams(dimension_semantics=sem, vmem_limit_bytes=VMEM_LIMIT)


def _sigmoid(x):
    return 1.0 / (1.0 + jnp.exp(-x))


def _silu(x):
    return x * _sigmoid(x)


def _log_sigmoid(x):
    return jnp.minimum(x, 0.0) - jnp.log(1.0 + jnp.exp(-jnp.abs(x)))


def _split3(x):
    hi = x.astype(BF16)
    r1 = x - hi.astype(F32)
    mid = r1.astype(BF16)
    lo = (r1 - mid.astype(F32)).astype(BF16)
    return hi, mid, lo


def _dot(a, b):
    return jnp.dot(a, b, preferred_element_type=F32)


def _dot_nt(a, b):
    return lax.dot_general(a, b, (((1,), (1,)), ((), ())), preferred_element_type=F32)


def _ada_kernel(c_ref, w_ref, b_ref, o_ref):
    c = c_ref[...]
    a = _silu(c).astype(BF16)
    o_ref[...] = _dot(a, w_ref[...].astype(BF16)) + b_ref[...]


def ada_mod(c_all, w_ada, b_ada):
    L, D, N = w_ada.shape
    R = c_all.shape[0]
    tn = 768
    return pl.pallas_call(
        _ada_kernel,
        grid=(L, N // tn),
        in_specs=[pl.BlockSpec((R, D), lambda l, j: (0, 0)),
                  pl.BlockSpec((None, D, tn), lambda l, j: (l, 0, j)),
                  pl.BlockSpec((None, 1, tn), lambda l, j: (l, 0, j))],
        out_specs=pl.BlockSpec((None, R, tn), lambda l, j: (l, 0, j)),
        out_shape=jax.ShapeDtypeStruct((L, R, N), F32),
        compiler_params=_cparams("parallel", "parallel"),
        name="ada_mod",
    )(c_all, w_ada, b_ada.reshape(L, 1, N))


def _modulated_norm(x, nw, scale, shift):
    ms = jnp.mean(x * x, axis=-1, keepdims=True)
    h = (x * lax.rsqrt(ms + EPS) * nw) * (1.0 + scale) + shift
    return h.reshape(-1, h.shape[-1]).astype(BF16)


def _inproj_kernel(x_ref, nw_ref, sc_ref, sh_ref, w_ref, wf_ref, bf_ref,
                   proj_ref, k_ref, v_ref, logf_ref, h_scr):
    j = pl.program_id(1)

    @pl.when(j == 0)
    def _():
        hb = _modulated_norm(x_ref[...], nw_ref[...], sc_ref[...], sh_ref[...])
        h_scr[...] = hb
        logf_ref[...] = _log_sigmoid(_dot(hb, wf_ref[...]) + bf_ref[...])

    h = h_scr[...]
    for c in range(IN_TN // D_GRP):
        r = _dot(h, w_ref[:, c * D_GRP:(c + 1) * D_GRP])
        proj_ref[:, c * D_GRP:(c + 1) * D_GRP] = r.astype(BF16)
        for split, ref in ((S_CK, k_ref), (S_CV, v_ref)):
            if split % 3 == c:
                @pl.when(j == split // 3)
                def _(r=r, ref=ref):
                    ref[...] = r


def in_proj(x, norm_w, scale, shift, w_main, w_f, b_f, *, bb, tt):
    B, T, D = x.shape
    rows = bb * tt
    nb, nt = B // bb, T // tt
    R = B * T
    N = w_main.shape[1]

    def xmap(i, j):
        return (i // nt, i % nt, 0)

    def mmap(i, j):
        return (i // nt, 0, 0)

    return pl.pallas_call(
        _inproj_kernel,
        grid=(nb * nt, N // IN_TN),
        in_specs=[pl.BlockSpec((bb, tt, D), xmap),
                  pl.BlockSpec((1, 1, D), lambda i, j: (0, 0, 0)),
                  pl.BlockSpec((bb, 1, D), mmap),
                  pl.BlockSpec((bb, 1, D), mmap),
                  pl.BlockSpec((D, IN_TN), lambda i, j: (0, j)),
                  pl.BlockSpec((D, LANES), lambda i, j: (0, 0)),
                  pl.BlockSpec((1, LANES), lambda i, j: (0, 0))],
        out_specs=[pl.BlockSpec((rows, IN_TN), lambda i, j: (i, j)),
                   pl.BlockSpec((rows, D_GRP), lambda i, j: (i, 0)),
                   pl.BlockSpec((rows, D_GRP), lambda i, j: (i, 0)),
                   pl.BlockSpec((rows, LANES), lambda i, j: (i, 0))],
        out_shape=[jax.ShapeDtypeStruct((R, N), BF16),
                   jax.ShapeDtypeStruct((R, D_GRP), F32),
                   jax.ShapeDtypeStruct((R, D_GRP), F32),
                   jax.ShapeDtypeStruct((R, LANES), F32)],
        scratch_shapes=[pltpu.VMEM((rows, D), BF16)],
        compiler_params=_cparams("parallel", "arbitrary"),
        name="in_proj",
    )(x, norm_w.reshape(1, 1, D), scale, shift, w_main, w_f, b_f)


HA = 8
HB = 32

def _conv_kernel(ab_ref, ac_ref, ax_ref, az_ref, ba_ref, bg_ref, bz_ref, ha_ref, hb_ref,
                 wa_ref, wb_ref, bb_ref, lw_ref, lb_ref,
                 ya_ref, yb_ref, na_ref, nb_ref, seqa, seqb, *, tt):
    t = pl.program_id(1)
    na, nb = A_CONV - 1, B_CONV - 1

    @pl.when(t == 0)
    def _():
        seqa[HA - na:HA, :] = ha_ref[0]
        seqb[0, 0:HB - nb, :] = jnp.zeros((HB - nb, D_GRP), F32)
        seqb[0, HB - nb:HB, :] = hb_ref[0]

    @pl.when(t > 0)
    def _():
        ta = seqa[HA + tt - na:HA + tt, :]
        seqa[HA - na:HA, :] = ta
        tb = seqb[0, HB + tt - nb:HB + tt, :]
        seqb[0, HB - nb:HB, :] = tb

    seqa[HA:HA + tt, :] = ac_ref[0].astype(F32) * ax_ref[0].astype(F32)
    seqb[0, HB:HB + tt, :] = ba_ref[0].astype(F32) * _sigmoid(bg_ref[0].astype(F32))
    for r in range(1, 8):
        seqb[r, 0:HB + tt - r, :] = seqb[0, r:HB + tt, :]

    rc = min(tt, 32)
    for r0 in range(0, tt, rc):
        acc = wa_ref[0:1, :] * seqa[HA - na + r0:HA - na + r0 + rc, :]
        for w in range(1, A_CONV):
            acc = acc + wa_ref[w:w + 1, :] * seqa[HA - na + r0 + w:HA - na + r0 + w + rc, :]
        ya = _silu(az_ref[0, r0:r0 + rc, :].astype(F32)) * ab_ref[0, r0:r0 + rc, :].astype(F32) * acc
        ya_ref[0, r0:r0 + rc, :] = ya.astype(BF16)

        acc = None
        for w in range(B_CONV):
            a8, r8 = divmod(HB - nb + w, 8)
            term = wb_ref[w:w + 1, :] * seqb[r8, 8 * a8 + r0:8 * a8 + r0 + rc, :]
            acc = term if acc is None else acc + term
        y = acc + bb_ref[...]
        mu = jnp.mean(y, axis=-1, keepdims=True)
        yc = y - mu
        var = jnp.mean(yc * yc, axis=-1, keepdims=True)
        yn = yc * lax.rsqrt(var + EPS) * lw_ref[...] + lb_ref[...]
        yb = _silu(bz_ref[0, r0:r0 + rc, :].astype(F32)) * _silu(yn)
        yb_ref[0, r0:r0 + rc, :] = yb.astype(BF16)

    na_ref[0] = seqa[HA + tt - na:HA + tt, :]
    nb_ref[0] = seqb[0, HB + tt - nb:HB + tt, :]


def conv_ab(proj3, hist_a, hist_b, wa, wb, bias_b, ln_w, ln_b, *, tt):
    B, T, _ = proj3.shape
    nt = T // tt

    def col(g):
        return pl.BlockSpec((1, tt, D_GRP), lambda b, t, g=g: (b, t, g))

    def full(a):
        return pl.BlockSpec(a.shape, lambda b, t: (0,) * a.ndim)

    bias_b, ln_w, ln_b = (a.reshape(1, D_GRP) for a in (bias_b, ln_w, ln_b))
    return pl.pallas_call(
        functools.partial(_conv_kernel, tt=tt),
        grid=(B, nt),
        in_specs=[col(S_AB), col(S_AC), col(S_AX), col(S_AZ), col(S_BA), col(S_BG), col(S_BZ),
                  pl.BlockSpec((1, A_CONV - 1, D_GRP), lambda b, t: (b, 0, 0)),
                  pl.BlockSpec((1, B_CONV - 1, D_GRP), lambda b, t: (b, 0, 0)),
                  full(wa), full(wb), full(bias_b), full(ln_w), full(ln_b)],
        out_specs=[pl.BlockSpec((1, tt, D_GRP), lambda b, t: (b, t, 0)),
                   pl.BlockSpec((1, tt, D_GRP), lambda b, t: (b, t, 0)),
                   pl.BlockSpec((1, A_CONV - 1, D_GRP), lambda b, t: (b, 0, 0)),
                   pl.BlockSpec((1, B_CONV - 1, D_GRP), lambda b, t: (b, 0, 0))],
        out_shape=[jax.ShapeDtypeStruct((B, T, D_GRP), BF16),
                   jax.ShapeDtypeStruct((B, T, D_GRP), BF16),
                   jax.ShapeDtypeStruct((B, A_CONV - 1, D_GRP), F32),
                   jax.ShapeDtypeStruct((B, B_CONV - 1, D_GRP), F32)],
        scratch_shapes=[pltpu.VMEM((HA + tt, D_GRP), F32), pltpu.VMEM((8, HB + tt, D_GRP), F32)],
        compiler_params=_cparams("parallel", "arbitrary"),
        name="conv_ab",
    )(proj3, proj3, proj3, proj3, proj3, proj3, proj3, hist_a, hist_b, wa, wb, bias_b, ln_w, ln_b)


def _cumsum_kernel(x_ref, o_ref, carry, *, cb, stride):
    j = pl.program_id(0)

    @pl.when(j == 0)
    def _():
        carry[...] = jnp.zeros_like(carry)

    lane = lax.broadcasted_iota(jnp.int32, carry.shape, 1)
    head = jnp.where(lane < stride, pltpu.roll(carry[...], stride, 1), 0.0)
    x = x_ref[...]
    x = jnp.concatenate([x[:, 0:LANES] + head, x[:, LANES:]], axis=1) if cb > LANES else x + head
    hi, mid, lo = _split3(x)
    r = lax.broadcasted_iota(jnp.int32, (cb, cb), 0)
    c = lax.broadcasted_iota(jnp.int32, (cb, cb), 1)
    same_series = lax.bitwise_and(r, stride - 1) == lax.bitwise_and(c, stride - 1)
    u = ((r <= c) & same_series).astype(BF16)
    cs = _dot(hi, u) + _dot(mid, u) + _dot(lo, u)
    o_ref[...] = cs
    carry[...] = cs[:, cb - LANES:cb]


def cumsum_lanes(x, stride=1):
    R, L = x.shape
    cb = 256 if L % 256 == 0 else LANES
    return pl.pallas_call(
        functools.partial(_cumsum_kernel, cb=cb, stride=stride),
        grid=(L // cb,),
        in_specs=[pl.BlockSpec((R, cb), lambda j: (0, j))],
        out_specs=pl.BlockSpec((R, cb), lambda j: (0, j)),
        out_shape=jax.ShapeDtypeStruct((R, L), F32),
        scratch_shapes=[pltpu.VMEM((R, LANES), F32)],
        compiler_params=_cparams("arbitrary"),
        name="cumsum_lanes",
    )(x)


LOG2E = 1.4426950408889634
ATT_PREP_ROWS = 1024
ATT_SPLIT = 2


def _bias_lanes(d, key_side):
    hi, mid, lo = (p.astype(F32) for p in _split3(d))
    lane = lax.broadcasted_iota(jnp.int32, (d.shape[0], LANES), 1)
    if key_side:
        a = jnp.where(lane == 3, -hi, jnp.where(lane == 4, -mid, jnp.where(lane == 5, -lo, 0.0)))
        a = jnp.where(lane < 3, 1.0, a)
    else:
        a = jnp.where(lane == 0, hi, jnp.where(lane == 1, mid, jnp.where(lane == 2, lo, 0.0)))
        a = jnp.where((lane >= 3) & (lane < 6), 1.0, a)
    return a.astype(BF16)


def _fox_prompt_kernel(q_ref, k_ref, v_ref, z_ref, dq_ref, dk_ref, y_ref,
                       ka_scr, va_scr, qa_scr, s0_scr, s1_scr, p_scr, alpha_scr, m_scr, l_scr, acc_scr, *, tile):
    qi = pl.program_id(1)
    T = k_ref.shape[0]
    th = tile // ATT_SPLIT

    @pl.when(qi == 0)
    def _():
        pr = min(T, ATT_PREP_ROWS)
        one_lane = (lax.broadcasted_iota(jnp.int32, (pr, LANES), 1) == 0).astype(BF16)

        def prep(i, _):
            rows = pl.ds(pl.multiple_of(i * pr, pr), pr)
            ka_scr[rows, 0:DH] = k_ref[rows, :]
            ka_scr[rows, DH:2 * DH] = _bias_lanes(dk_ref[rows, :] * LOG2E, True)
            va_scr[rows, 0:DH] = v_ref[rows, :]
            va_scr[rows, DH:2 * DH] = one_lane
            return 0

        lax.fori_loop(0, T // pr, prep, 0)

    qa_scr[:, 0:DH] = (q_ref[...].astype(F32) * (DH ** -0.5 * LOG2E)).astype(BF16)
    qa_scr[:, DH:2 * DH] = _bias_lanes(dq_ref[...] * LOG2E, False)

    m_scr[...] = jnp.full(m_scr.shape, -jnp.inf, F32)
    l_scr[...] = jnp.zeros(l_scr.shape, F32)
    acc_scr[...] = jnp.zeros(acc_scr.shape, F32)

    def logits(ki, s_buf):
        off = pl.multiple_of(ki * tile, tile)
        ka = ka_scr[pl.ds(off, tile), :]
        for g in range(ATT_SPLIT):
            rows = slice(g * th, (g + 1) * th)
            s_buf[rows, :] = _dot_nt(qa_scr[rows, :], ka)

    def softmax(s_buf, masked):
        for g in range(ATT_SPLIT):
            rows = slice(g * th, (g + 1) * th)
            s = s_buf[rows, :]
            if masked:
                r = lax.broadcasted_iota(jnp.int32, (th, tile), 0) + g * th
                c = lax.broadcasted_iota(jnp.int32, (th, tile), 1)
                s = jnp.where(c <= r, s, -jnp.inf)
            m = m_scr[rows, :]
            m_new = jnp.maximum(m, jnp.max(s, axis=-1, keepdims=True))
            alpha_scr[rows, :] = jnp.exp2(m - m_new)
            p_scr[rows, :] = jnp.exp2(s - m_new).astype(BF16)
            m_scr[rows, :] = m_new

    def weighted_values(ki):
        off = pl.multiple_of(ki * tile, tile)
        va = va_scr[pl.ds(off, tile), :]
        for g in range(ATT_SPLIT):
            rows = slice(g * th, (g + 1) * th)
            pv = _dot(p_scr[rows, :], va)
            alpha = alpha_scr[rows, :]
            l_scr[rows, :] = alpha * l_scr[rows, :] + pv[:, DH:DH + 1]
            acc_scr[rows, :] = alpha * acc_scr[rows, :] + pv[:, 0:DH]

    p_scr[...] = jnp.zeros(p_scr.shape, BF16)
    alpha_scr[...] = jnp.ones(alpha_scr.shape, F32)
    s_bufs = (s0_scr, s1_scr)
    logits(0, s_bufs[0])

    def stage(ki, parity, masked):
        weighted_values(jnp.maximum(ki - 1, 0))
        softmax(s_bufs[parity], masked)
        if not masked:
            logits(ki + 1, s_bufs[1 - parity])

    def pair(j, _):
        stage(2 * j, 0, False)
        stage(2 * j + 1, 1, False)
        return 0

    lax.fori_loop(0, lax.shift_right_logical(qi, 1), pair, 0)
    odd = lax.bitwise_and(qi, 1) == 1

    @pl.when(odd)
    def _():
        stage(qi - 1, 0, False)
        stage(qi, 1, True)

    @pl.when(jnp.logical_not(odd))
    def _():
        stage(qi, 0, True)

    weighted_values(qi)
    y_ref[...] = (_silu(z_ref[...].astype(F32)) * (acc_scr[...] / l_scr[...])).astype(BF16)


def fox_prompt(proj, dcol, *, tile):
    T = proj.shape[0]
    cpb = D_GRP // DH
    return pl.pallas_call(
        functools.partial(_fox_prompt_kernel, tile=tile),
        grid=(H, T // tile),
        in_specs=[pl.BlockSpec((tile, DH), lambda h, i: (i, S_CQ * cpb + h)),
                  pl.BlockSpec((T, DH), lambda h, i: (0, S_CK * cpb + h)),
                  pl.BlockSpec((T, DH), lambda h, i: (0, S_CV * cpb + h)),
                  pl.BlockSpec((tile, DH), lambda h, i: (i, S_CZ * cpb + h)),
                  pl.BlockSpec((None, tile, 1), lambda h, i: (h, i, 0)),
                  pl.BlockSpec((None, T, 1), lambda h, i: (h, 0, 0))],
        out_specs=pl.BlockSpec((tile, DH), lambda h, i: (i, h)),
        out_shape=jax.ShapeDtypeStruct((T, D_GRP), BF16),
        scratch_shapes=[pltpu.VMEM((T, 2 * DH), BF16), pltpu.VMEM((T, 2 * DH), BF16),
                        pltpu.VMEM((tile, 2 * DH), BF16), pltpu.VMEM((tile, tile), F32),
                        pltpu.VMEM((tile, tile), F32), pltpu.VMEM((tile, tile), BF16), pltpu.VMEM((tile, 1), F32),
                        pltpu.VMEM((tile, 1), F32), pltpu.VMEM((tile, 1), F32), pltpu.VMEM((tile, DH), F32)],
        compiler_params=_cparams("parallel", "arbitrary"),
        name="fox_prompt",
    )(proj, proj, proj, proj, dcol, dcol)


def _fox_sample_kernel(q_ref, kn_ref, vn_ref, z_ref, kc_ref, vc_ref, dkh_ref, dq_ref, dkn_ref,
                       y_ref, m_scr, l_scr, acc_scr, *, tq):
    kt = pl.program_id(1)
    nk = pl.num_programs(1)
    scale = DH ** -0.5

    @pl.when(kt == 0)
    def _():
        m_scr[...] = jnp.full(m_scr.shape, -jnp.inf, F32)
        l_scr[...] = jnp.zeros(l_scr.shape, F32)
        acc_scr[...] = jnp.zeros(acc_scr.shape, F32)

    def update(rows, s, v):
        m = m_scr[rows, :]
        m_new = jnp.maximum(m, jnp.max(s, axis=-1, keepdims=True))
        alpha = jnp.exp(m - m_new)
        p = jnp.exp(s - m_new)
        l_scr[rows, :] = alpha * l_scr[rows, :] + jnp.sum(p, axis=-1, keepdims=True)
        acc_scr[rows, :] = alpha * acc_scr[rows, :] + _dot(p.astype(BF16), v)
        m_scr[rows, :] = m_new

    heads = [slice(h * DH, (h + 1) * DH) for h in range(H)]
    q_stack = jnp.concatenate([q_ref[0, :, cs] for cs in heads], axis=0)
    dq_stack = jnp.concatenate([dq_ref[0, :, h:h + 1] for h in range(H)], axis=0)
    kf = kc_ref[0].astype(BF16)
    n = kf.shape[0]
    s = _dot_nt(q_stack, kf) * scale + (dq_stack - dkh_ref[0])
    row_head = lax.shift_right_logical(lax.broadcasted_iota(jnp.int32, (H * tq, n), 0), tq.bit_length() - 1)
    col_head = lax.bitwise_and(lax.broadcasted_iota(jnp.int32, (H * tq, n), 1), H - 1)
    update(slice(None), jnp.where(row_head == col_head, s, -jnp.inf), vc_ref[0].astype(BF16))

    @pl.when(kt == nk - 1)
    def _():
        r = lax.broadcasted_iota(jnp.int32, (tq, tq), 0)
        c = lax.broadcasted_iota(jnp.int32, (tq, tq), 1)
        for h in range(H):
            rows = slice(h * tq, (h + 1) * tq)
            s_new = _dot_nt(q_ref[0, :, heads[h]], kn_ref[0, :, heads[h]]) * scale \
                + (dq_ref[0, :, h:h + 1] - dkn_ref[0, h:h + 1, :])
            update(rows, jnp.where(c <= r, s_new, -jnp.inf), vn_ref[0, :, heads[h]])
            o = acc_scr[rows, :] / l_scr[rows, :]
            y_ref[0, :, heads[h]] = (_silu(z_ref[0, :, heads[h]].astype(F32)) * o).astype(BF16)


def fox_sample(proj3, cache_k, cache_v, layer, dkh, dq_col, dkn_row, *, tk):
    B, tq, _ = proj3.shape
    n = tk * H
    assert tq & (tq - 1) == 0 and H & (H - 1) == 0

    def col(g):
        return pl.BlockSpec((1, tq, D_GRP), lambda b, t, g=g: (b, 0, g))

    cache_spec = pl.BlockSpec((None, 1, n, DH), lambda b, t: (layer, b, t, 0))
    return pl.pallas_call(
        functools.partial(_fox_sample_kernel, tq=tq),
        grid=(B, cache_k.shape[2] // n),
        in_specs=[col(S_CQ), col(S_CK), col(S_CV), col(S_CZ), cache_spec, cache_spec,
                  pl.BlockSpec((None, 1, 1, n), lambda b, t: (layer, b, 0, t)),
                  pl.BlockSpec((1, tq, H), lambda b, t: (b, 0, 0)),
                  pl.BlockSpec((1, H, tq), lambda b, t: (b, 0, 0))],
        out_specs=pl.BlockSpec((1, tq, D_GRP), lambda b, t: (b, 0, 0)),
        out_shape=jax.ShapeDtypeStruct((B, tq, D_GRP), BF16),
        scratch_shapes=[pltpu.VMEM((H * tq, 1), F32), pltpu.VMEM((H * tq, 1), F32),
                        pltpu.VMEM((H * tq, DH), F32)],
        compiler_params=_cparams("parallel", "arbitrary"),
        name="fox_sample",
    )(proj3, proj3, proj3, proj3, cache_k, cache_v, dkh, dq_col, dkn_row)


def _lower_bound(lbl_ref, layer):
    x = lbl_ref[...]
    e = jnp.exp(x - jnp.max(x, axis=0, keepdims=True))
    sm = e / jnp.sum(e, axis=0, keepdims=True)
    cum = sm[0:1, :]
    first = cum
    for i in range(1, layer + 1):
        cum = cum + sm[i:i + 1, :]
    return cum - first


def _gla_levels(C):
    return [C >> i for i in range(1, C.bit_length())]


def _gla_group(q, gl, vi, lb, st_list, b_scr, *, C):
    G = len(st_list)
    N = G * C
    sig = _sigmoid(gl)
    g = jnp.log(lb + (1.0 - lb) * sig)
    kk = (1.0 - lb) * (1.0 - sig)

    row = lax.broadcasted_iota(jnp.int32, (N, N), 0)
    colm = lax.broadcasted_iota(jnp.int32, (N, N), 1)

    def same_block(size):
        sh = size.bit_length() - 1
        return lax.shift_right_logical(row, sh) == lax.shift_right_logical(colm, sh)

    tril = (same_block(C) & (colm <= row)).astype(BF16)
    g_hi, g_mid, g_lo = _split3(g)
    b = _dot(tril, g_hi) + _dot(tril, g_mid) + _dot(tril, g_lo)
    b_scr[...] = b

    def gather_rows(size, offset):
        parts = [jnp.broadcast_to(b_scr[i * size + offset:i * size + offset + 1, :], (size, LANES))
                 for i in range(N // size)]
        return parts[0] if len(parts) == 1 else jnp.concatenate(parts, axis=0)

    trow = lax.broadcasted_iota(jnp.int32, (N, LANES), 0)
    qb16 = q.astype(BF16)
    kb16 = kk.astype(BF16)
    a = jnp.where(row == colm, _dot_nt(qb16, kb16), 0.0)
    for m in _gla_levels(C):
        size = 2 * m
        u = lax.bitwise_and(trow, size - 1)
        upper = u >= m
        if size >= 8:
            ref = gather_rows(size, m - 1)
        elif size == 4:
            ref = jnp.where(u == 0, pltpu.roll(b, N - 1, 0),
                            jnp.where(u == 1, b, jnp.where(u == 2, pltpu.roll(b, 1, 0), pltpu.roll(b, 2, 0))))
        else:
            ref = jnp.where(u == 1, pltpu.roll(b, 1, 0), b)
        x = jnp.exp(jnp.where(upper, b - ref, ref - b))
        qm = jnp.where(upper, q * x, 0.0).astype(BF16)
        km = jnp.where(upper, 0.0, kk * x).astype(BF16)
        a = a + jnp.where(same_block(size), _dot_nt(qm, km), 0.0)

    vb16 = vi.astype(BF16)
    o = _dot(a.astype(BF16), vb16)

    qin = (q * jnp.exp(b)).astype(BF16)
    kd = kk * jnp.exp(gather_rows(C, C - 1) - b)
    vt = vi.T.astype(BF16)
    o_parts, st_new = [], []
    for gi in range(G):
        st = st_list[gi]
        o_parts.append(_dot_nt(qin[gi * C:(gi + 1) * C, :], st.astype(BF16)))
        seg = lax.shift_right_logical(trow, C.bit_length() - 1) == gi
        kd_g = jnp.where(seg, kd, 0.0).astype(BF16)
        ebl = jnp.exp(b_scr[(gi + 1) * C - 1:(gi + 1) * C, :])
        st_new.append(st * ebl + _dot(vt, kd_g))
    o = o + (o_parts[0] if G == 1 else jnp.concatenate(o_parts, axis=0))
    return o, st_new


def _gla_out(o, z, dn):
    ms = jnp.mean(o * o, axis=-1, keepdims=True)
    return (_silu(z) * (o * lax.rsqrt(ms + EPS) * dn)).astype(BF16)


def _hgrn_prompt_kernel(q_ref, f_ref, i_ref, z_ref, lbl_ref, dn_ref, y_ref, s_ref, st_scr, b_scr,
                        *, layer, nseq):
    step = pl.program_id(0)
    C = CHUNK

    @pl.when(step == 0)
    def _():
        st_scr[...] = jnp.zeros(st_scr.shape, F32)

    lb_row = _lower_bound(lbl_ref, layer)
    lb = jnp.concatenate([jnp.broadcast_to(lb_row[:, h * DH:(h + 1) * DH], (C, DH)) for h in range(H)], axis=0)
    dn = dn_ref[...]

    def stack(ref, r0):
        return jnp.concatenate([ref[r0:r0 + C, h * DH:(h + 1) * DH].astype(F32) for h in range(H)], axis=0)

    for cidx in range(nseq):
        r0 = cidx * C
        st_list = [st_scr[h] for h in range(H)]
        o, st_new = _gla_group(stack(q_ref, r0), stack(f_ref, r0), stack(i_ref, r0), lb, st_list, b_scr, C=C)
        y = _gla_out(o, stack(z_ref, r0), dn)
        for h in range(H):
            st_scr[h] = st_new[h]
            y_ref[r0:r0 + C, h * DH:(h + 1) * DH] = y[h * C:(h + 1) * C, :]

    @pl.when(step == pl.num_programs(0) - 1)
    def _():
        for h in range(H):
            s_ref[h] = st_scr[h].T


def hgrn_prompt(proj, lb_logits, d_norm, layer, *, nseq=4):
    T = proj.shape[0]
    rows = nseq * CHUNK

    def col(g):
        return pl.BlockSpec((rows, D_GRP), lambda i, g=g: (i, g))

    return pl.pallas_call(
        functools.partial(_hgrn_prompt_kernel, layer=layer, nseq=nseq),
        grid=(T // rows,),
        in_specs=[col(S_DQ), col(S_DF), col(S_DI), col(S_DZ),
                  pl.BlockSpec(lb_logits.shape, lambda i: (0, 0)),
                  pl.BlockSpec((1, DH), lambda i: (0, 0))],
        out_specs=[pl.BlockSpec((rows, D_GRP), lambda i: (i, 0)),
                   pl.BlockSpec((H, DH, DH), lambda i: (0, 0, 0))],
        out_shape=[jax.ShapeDtypeStruct((T, D_GRP), BF16),
                   jax.ShapeDtypeStruct((H, DH, DH), F32)],
        scratch_shapes=[pltpu.VMEM((H, DH, DH), F32), pltpu.VMEM((H * CHUNK, DH), F32)],
        compiler_params=_cparams("arbitrary"),
        name="hgrn_prompt",
    )(proj, proj, proj, proj, lb_logits, d_norm.reshape(1, DH))


def _hgrn_sample_kernel(q_ref, f_ref, i_ref, z_ref, s0_ref, lbl_ref, dn_ref, y_ref, s_ref, b_scr,
                        *, layer, nb, C):
    lb_row = _lower_bound(lbl_ref, layer)
    segs = [(bi, h) for bi in range(nb) for h in range(H)]
    lb = jnp.concatenate([jnp.broadcast_to(lb_row[:, h * DH:(h + 1) * DH], (C, DH)) for _, h in segs], axis=0)

    def stack(ref):
        return jnp.concatenate([ref[bi, :, h * DH:(h + 1) * DH].astype(F32) for bi, h in segs], axis=0)

    st_list = [s0_ref[bi, h].T for bi, h in segs]
    o, st_new = _gla_group(stack(q_ref), stack(f_ref), stack(i_ref), lb, st_list, b_scr, C=C)
    y = _gla_out(o, stack(z_ref), dn_ref[...])
    for gi, (bi, h) in enumerate(segs):
        s_ref[bi, h] = st_new[gi].T
        y_ref[bi, :, h * DH:(h + 1) * DH] = y[gi * C:(gi + 1) * C, :]


def hgrn_sample(proj3, state, lb_logits, d_norm, layer, *, nb=4):
    B, C, _ = proj3.shape

    def col(g):
        return pl.BlockSpec((nb, C, D_GRP), lambda i, g=g: (i, 0, g))

    return pl.pallas_call(
        functools.partial(_hgrn_sample_kernel, layer=layer, nb=nb, C=C),
        grid=(B // nb,),
        in_specs=[col(S_DQ), col(S_DF), col(S_DI), col(S_DZ),
                  pl.BlockSpec((None, nb, H, DH, DH), lambda i: (layer, i, 0, 0, 0)),
                  pl.BlockSpec(lb_logits.shape, lambda i: (0, 0)),
                  pl.BlockSpec((1, DH), lambda i: (0, 0))],
        out_specs=[pl.BlockSpec((nb, C, D_GRP), lambda i: (i, 0, 0)),
                   pl.BlockSpec((nb, H, DH, DH), lambda i: (i, 0, 0, 0))],
        out_shape=[jax.ShapeDtypeStruct((B, C, D_GRP), BF16),
                   jax.ShapeDtypeStruct((B, H, DH, DH), F32)],
        scratch_shapes=[pltpu.VMEM((nb * H * C, DH), F32)],
        compiler_params=_cparams("parallel"),
        name="hgrn_sample",
    )(proj3, proj3, proj3, proj3, state, lb_logits, d_norm.reshape(1, DH))


def _outproj_kernel(ya_ref, yb_ref, yc_ref, yd_ref, w_ref, x_ref, g_ref, fw_ref, o_ref, *, final):
    acc = _dot(ya_ref[...].reshape(-1, D_GRP), w_ref[0:D_GRP, :])
    for i, ref in enumerate((yb_ref, yc_ref, yd_ref), start=1):
        acc = acc + _dot(ref[...].reshape(-1, D_GRP), w_ref[i * D_GRP:(i + 1) * D_GRP, :])
    x = x_ref[...]
    xn = x + g_ref[...] * acc.reshape(x.shape)
    if final:
        ms = jnp.mean(xn * xn, axis=-1, keepdims=True)
        xn = xn * lax.rsqrt(ms + EPS) * fw_ref[...]
    o_ref[...] = xn


def out_proj(ya, yb, yc, yd, w_out, x, gate, final_w, *, bb, tt, final):
    B, T, D = x.shape
    nt = T // tt

    def ymap(i):
        return (i // nt, i % nt, 0)

    yspec = pl.BlockSpec((bb, tt, D_GRP), ymap)
    return pl.pallas_call(
        functools.partial(_outproj_kernel, final=final),
        grid=((B // bb) * nt,),
        in_specs=[yspec, yspec, yspec, yspec,
                  pl.BlockSpec(w_out.shape, lambda i: (0, 0)),
                  pl.BlockSpec((bb, tt, D), ymap),
                  pl.BlockSpec((bb, 1, D), lambda i: (i // nt, 0, 0)),
                  pl.BlockSpec((1, 1, D), lambda i: (0, 0, 0))],
        out_specs=pl.BlockSpec((bb, tt, D), ymap),
        out_shape=jax.ShapeDtypeStruct((B, T, D), F32),
        compiler_params=_cparams("parallel"),
        name="out_proj",
    )(ya, yb, yc, yd, w_out, x, gate, final_w.reshape(1, 1, D))


def kernel(x_prompt, x_sample, c_prompt, c_sample, cache_a_conv, cache_b_conv, cache_k, cache_v, cache_logf, state_hgrn, norm_w, w_ada, b_ada, w_in, b_f, a_conv_w, b_conv_w, b_conv_b, b_ln_w, b_ln_b, d_norm_w, hgrn_lb_logits, w_out, final_norm_w):
    L = DEPTH
    Bp, Tp, D = x_prompt.shape
    Bs, Ts, _ = x_sample.shape
    P = cache_k.shape[2]
    assert Bp == 1

    nf = 11 * D_GRP
    w_in_b = w_in.astype(BF16)
    w_main = jnp.concatenate([w_in_b[:, :, :nf], w_in_b[:, :, nf + H:]], axis=-1)
    w_f = jnp.pad(w_in_b[:, :, nf:nf + H], ((0, 0), (0, 0), (0, LANES - H)))
    bf_pad = jnp.pad(b_f, ((0, 0), (0, LANES - H))).reshape(L, 1, LANES)
    w_out_b = w_out.astype(BF16)

    nc = Bp + Bs
    c_all = jnp.pad(jnp.concatenate([c_prompt, c_sample], axis=0), ((0, (-nc) % 8), (0, 0)))
    mod = ada_mod(c_all, w_ada, b_ada)

    cache_kf = cache_k.reshape(L, Bs, P * H, DH)
    cache_vf = cache_v.reshape(L, Bs, P * H, DH)
    pad_rows = (-(L * Bs)) % 8
    hist_lf = jnp.pad(cache_logf.reshape(L * Bs, P * H), ((0, pad_rows), (0, 0)))
    dkh_all = cumsum_lanes(hist_lf, stride=H)[:L * Bs].reshape(L, Bs, 1, P * H)
    hist_total = dkh_all[:, :, :, (P - 1) * H:]

    xp, xs = x_prompt, x_sample
    zeros_a = jnp.zeros((Bp, A_CONV - 1, D_GRP), F32)
    zeros_b = jnp.zeros((Bp, B_CONV - 1, D_GRP), F32)
    outs_p = [[] for _ in range(6)]
    outs_s = [[] for _ in range(6)]

    for l in range(L):
        last = l == L - 1
        shift_p, scale_p, gate_p = (mod[l, :Bp, i * D:(i + 1) * D].reshape(Bp, 1, D) for i in range(3))
        shift_s, scale_s, gate_s = (mod[l, Bp:nc, i * D:(i + 1) * D].reshape(Bs, 1, D) for i in range(3))

        proj, k_p, v_p, lf_p = in_proj(xp, norm_w[l], scale_p, shift_p, w_main[l], w_f[l], bf_pad[l],
                                       bb=1, tt=min(IN_TM, Tp))
        ya, yb, na_p, nb_p = conv_ab(proj.reshape(Bp, Tp, -1), zeros_a, zeros_b, a_conv_w[l], b_conv_w[l],
                                     b_conv_b[l], b_ln_w[l], b_ln_b[l], tt=CONV_TT)
        lf_t = jnp.pad(lf_p[:, :H].T, ((0, 8 - H), (0, 0)))
        dcum = cumsum_lanes(lf_t)[:H]
        yc = fox_prompt(proj, dcum.reshape(H, Tp, 1), tile=ATT_T)
        yd, s_p = hgrn_prompt(proj, hgrn_lb_logits, d_norm_w[l], l)
        xp = out_proj(ya, yb, yc.reshape(Bp, Tp, D_GRP), yd.reshape(Bp, Tp, D_GRP), w_out_b[l], xp, gate_p,
                      final_norm_w, bb=1, tt=OUT_TM, final=last)
        for i, a in enumerate((na_p, nb_p, k_p.reshape(Bp, Tp, H, DH), v_p.reshape(Bp, Tp, H, DH),
                               lf_p[:, :H].reshape(Bp, Tp, H), s_p.reshape(Bp, H, DH, DH))):
            outs_p[i].append(a)

        proj, k_s, v_s, lf_s = in_proj(xs, norm_w[l], scale_s, shift_s, w_main[l], w_f[l], bf_pad[l],
                                       bb=Bs, tt=Ts)
        proj3 = proj.reshape(Bs, Ts, -1)
        ya, yb, na_s, nb_s = conv_ab(proj3, cache_a_conv[l], cache_b_conv[l], a_conv_w[l], b_conv_w[l],
                                     b_conv_b[l], b_ln_w[l], b_ln_b[l], tt=Ts)
        lf_new = lf_s[:, :H].reshape(Bs, Ts, H)
        lf_rows = jnp.pad(lf_new.reshape(Bs, Ts * H), ((0, (-Bs) % 8), (0, LANES - Ts * H)))
        dnew = cumsum_lanes(lf_rows, stride=H)[:Bs, :Ts * H].reshape(Bs, Ts, H) + hist_total[l]
        yc = fox_sample(proj3, cache_kf, cache_vf, l, dkh_all, dnew, jnp.transpose(dnew, (0, 2, 1)),
                        tk=min(ATT_TK_CACHE, P))
        yd, s_s = hgrn_sample(proj3, state_hgrn, hgrn_lb_logits, d_norm_w[l], l)
        xs = out_proj(ya, yb, yc, yd, w_out_b[l], xs, gate_s, final_norm_w, bb=Bs, tt=Ts, final=last)
        for i, a in enumerate((na_s, nb_s, k_s.reshape(Bs, Ts, H, DH), v_s.reshape(Bs, Ts, H, DH),
                               lf_new, s_s)):
            outs_s[i].append(a)

    res = [xp, xs]
    for i in range(6):
        res.append(jnp.stack(outs_p[i]))
        res.append(jnp.stack(outs_s[i]))
    return tuple(res)
```

```python
import functools

import jax
import jax.numpy as jnp
from jax import lax
from jax.experimental import pallas as pl
from jax.experimental.pallas import tpu as pltpu

F32 = jnp.float32
BF16 = jnp.bfloat16

DEPTH = 4
D_MODEL = 2048
D_GRP = 512
N_SPLIT = 15
H = 4
DH = 128
A_CONV = 3
B_CONV = 31
CHUNK = 64
EPS = 1e-6
LANES = 128
VMEM_LIMIT = 56 * 1024 * 1024

(S_AB, S_AC, S_AX, S_AZ, S_BA, S_BG, S_BZ, S_CQ, S_CK, S_CV, S_CZ, S_DQ, S_DF, S_DI, S_DZ) = range(15)

IN_TN = 1536
IN_TM = 1024
CONV_TT = 256
ATT_T = 512
ATT_TK_CACHE = 4096
OUT_TM = 512


def _cparams(*sem):
    return pltpu.CompilerParams(dimension_semantics=sem, vmem_limit_bytes=VMEM_LIMIT)


def _sigmoid(x):
    return 1.0 / (1.0 + jnp.exp(-x))


def _silu(x):
    return x * _sigmoid(x)


def _log_sigmoid(x):
    return jnp.minimum(x, 0.0) - jnp.log(1.0 + jnp.exp(-jnp.abs(x)))


def _split3(x):
    hi = x.astype(BF16)
    r1 = x - hi.astype(F32)
    mid = r1.astype(BF16)
    lo = (r1 - mid.astype(F32)).astype(BF16)
    return hi, mid, lo


def _dot(a, b):
    return jnp.dot(a, b, preferred_element_type=F32)


def _dot_nt(a, b):
    return lax.dot_general(a, b, (((1,), (1,)), ((), ())), preferred_element_type=F32)


def _ada_kernel(c_ref, w_ref, b_ref, o_ref):
    c = c_ref[...]
    a = _silu(c).astype(BF16)
    o_ref[...] = _dot(a, w_ref[...].astype(BF16)) + b_ref[...]


def ada_mod(c_all, w_ada, b_ada):
    L, D, N = w_ada.shape
    R = c_all.shape[0]
    tn = 768
    return pl.pallas_call(
        _ada_kernel,
        grid=(L, N // tn),
        in_specs=[pl.BlockSpec((R, D), lambda l, j: (0, 0)),
                  pl.BlockSpec((None, D, tn), lambda l, j: (l, 0, j)),
                  pl.BlockSpec((None, 1, tn), lambda l, j: (l, 0, j))],
        out_specs=pl.BlockSpec((None, R, tn), lambda l, j: (l, 0, j)),
        out_shape=jax.ShapeDtypeStruct((L, R, N), F32),
        compiler_params=_cparams("parallel", "parallel"),
        name="ada_mod",
    )(c_all, w_ada, b_ada.reshape(L, 1, N))


def _modulated_norm(x, nw, scale, shift):
    ms = jnp.mean(x * x, axis=-1, keepdims=True)
    h = (x * lax.rsqrt(ms + EPS) * nw) * (1.0 + scale) + shift
    return h.reshape(-1, h.shape[-1]).astype(BF16)


def _inproj_kernel(x_ref, nw_ref, sc_ref, sh_ref, w_ref, wf_ref, bf_ref, *rest):
    proj_ref, k_ref, v_ref, logf_ref, h_scr = rest[-5:]
    j = pl.program_id(1)

    @pl.when(j == 0)
    def _():
        hb = _modulated_norm(x_ref[...], nw_ref[...], sc_ref[...], sh_ref[...])
        h_scr[...] = hb
        logf_ref[...] = _log_sigmoid(_dot(hb, wf_ref[...]) + bf_ref[...])

    h = h_scr[...]
    for c in range(IN_TN // D_GRP):
        r = _dot(h, w_ref[:, c * D_GRP:(c + 1) * D_GRP])
        proj_ref[:, c * D_GRP:(c + 1) * D_GRP] = r.astype(BF16)
        for split, ref in ((S_CK, k_ref), (S_CV, v_ref)):
            if split % 3 == c:
                @pl.when(j == split // 3)
                def _(r=r, ref=ref):
                    ref[...] = r


SHIFT, SCALE, GATE = range(3)


def in_proj(x, layer, norm_w, mods, w_main, w_f, b_f, kv_bufs, *, bb, tt):
    B, T, D = x.shape
    L, _, N = w_main.shape
    rows = bb * tt
    nb, nt = B // bb, T // tt
    R = B * T

    def xmap(i, j):
        return (i // nt, i % nt, 0)

    def mod_spec(which):
        return pl.BlockSpec((None, None, bb, 1, D), lambda i, j: (layer, which, i // nt, 0, 0))

    kv_spec = pl.BlockSpec((None, rows, D_GRP), lambda i, j: (layer, i, 0))
    aliased = [] if kv_bufs is None else list(kv_bufs)
    n_in = 7
    return pl.pallas_call(
        _inproj_kernel,
        grid=(nb * nt, N // IN_TN),
        in_specs=[pl.BlockSpec((bb, tt, D), xmap),
                  pl.BlockSpec((1, 1, D), lambda i, j: (layer, 0, 0)),
                  mod_spec(SCALE), mod_spec(SHIFT),
                  pl.BlockSpec((None, D, IN_TN), lambda i, j: (layer, 0, j)),
                  pl.BlockSpec((None, D, LANES), lambda i, j: (layer, 0, 0)),
                  pl.BlockSpec((None, 1, LANES), lambda i, j: (layer, 0, 0))]
                 + [pl.BlockSpec(memory_space=pl.ANY)] * len(aliased),
        out_specs=[pl.BlockSpec((rows, IN_TN), lambda i, j: (i, j)), kv_spec, kv_spec,
                   pl.BlockSpec((rows, LANES), lambda i, j: (i, 0))],
        out_shape=[jax.ShapeDtypeStruct((R, N), BF16),
                   jax.ShapeDtypeStruct((L, R, D_GRP), F32),
                   jax.ShapeDtypeStruct((L, R, D_GRP), F32),
                   jax.ShapeDtypeStruct((R, LANES), F32)],
        input_output_aliases={n_in + a: 1 + a for a in range(len(aliased))},
        scratch_shapes=[pltpu.VMEM((rows, D), BF16)],
        compiler_params=_cparams("parallel", "arbitrary"),
        name="in_proj",
    )(x, norm_w, mods, mods, w_main, w_f, b_f, *aliased)


HA = 8
HB = 32

def _conv_kernel(ab_ref, ac_ref, ax_ref, az_ref, ba_ref, bg_ref, bz_ref, ha_ref, hb_ref,
                 wa_ref, wb_ref, bb_ref, lw_ref, lb_ref,
                 ya_ref, yb_ref, na_ref, nb_ref, seqa, seqb, *, tt):
    t = pl.program_id(1)
    na, nb = A_CONV - 1, B_CONV - 1

    @pl.when(t == 0)
    def _():
        seqa[HA - na:HA, :] = ha_ref[0]
        seqb[0, 0:HB - nb, :] = jnp.zeros((HB - nb, D_GRP), F32)
        seqb[0, HB - nb:HB, :] = hb_ref[0]

    @pl.when(t > 0)
    def _():
        ta = seqa[HA + tt - na:HA + tt, :]
        seqa[HA - na:HA, :] = ta
        tb = seqb[0, HB + tt - nb:HB + tt, :]
        seqb[0, HB - nb:HB, :] = tb

    seqa[HA:HA + tt, :] = ac_ref[0].astype(F32) * ax_ref[0].astype(F32)
    seqb[0, HB:HB + tt, :] = ba_ref[0].astype(F32) * _sigmoid(bg_ref[0].astype(F32))
    for r in range(1, 8):
        seqb[r, 0:HB + tt - r, :] = seqb[0, r:HB + tt, :]

    rc = min(tt, 32)
    for r0 in range(0, tt, rc):
        acc = wa_ref[0:1, :] * seqa[HA - na + r0:HA - na + r0 + rc, :]
        for w in range(1, A_CONV):
            acc = acc + wa_ref[w:w + 1, :] * seqa[HA - na + r0 + w:HA - na + r0 + w + rc, :]
        ya = _silu(az_ref[0, r0:r0 + rc, :].astype(F32)) * ab_ref[0, r0:r0 + rc, :].astype(F32) * acc
        ya_ref[0, r0:r0 + rc, :] = ya.astype(BF16)

        acc = None
        for w in range(B_CONV):
            a8, r8 = divmod(HB - nb + w, 8)
            term = wb_ref[w:w + 1, :] * seqb[r8, 8 * a8 + r0:8 * a8 + r0 + rc, :]
            acc = term if acc is None else acc + term
        y = acc + bb_ref[...]
        mu = jnp.mean(y, axis=-1, keepdims=True)
        yc = y - mu
        var = jnp.mean(yc * yc, axis=-1, keepdims=True)
        yn = yc * lax.rsqrt(var + EPS) * lw_ref[...] + lb_ref[...]
        yb = _silu(bz_ref[0, r0:r0 + rc, :].astype(F32)) * _silu(yn)
        yb_ref[0, r0:r0 + rc, :] = yb.astype(BF16)

    na_ref[0] = seqa[HA + tt - na:HA + tt, :]
    nb_ref[0] = seqb[0, HB + tt - nb:HB + tt, :]


def conv_ab(proj3, layer, hist_a, hist_b, hist_layer, wa, wb, bias_b, ln_w, ln_b, *, tt):
    B, T, _ = proj3.shape
    nt = T // tt

    def col(g):
        return pl.BlockSpec((1, tt, D_GRP), lambda b, t, g=g: (b, t, g))

    def per_layer(a):
        return pl.BlockSpec((None,) + a.shape[1:], lambda b, t: (layer,) + (0,) * (a.ndim - 1))

    return pl.pallas_call(
        functools.partial(_conv_kernel, tt=tt),
        grid=(B, nt),
        in_specs=[col(S_AB), col(S_AC), col(S_AX), col(S_AZ), col(S_BA), col(S_BG), col(S_BZ),
                  pl.BlockSpec((None, 1, A_CONV - 1, D_GRP), lambda b, t: (hist_layer, b, 0, 0)),
                  pl.BlockSpec((None, 1, B_CONV - 1, D_GRP), lambda b, t: (hist_layer, b, 0, 0)),
                  per_layer(wa), per_layer(wb), per_layer(bias_b), per_layer(ln_w), per_layer(ln_b)],
        out_specs=[pl.BlockSpec((1, tt, D_GRP), lambda b, t: (b, t, 0)),
                   pl.BlockSpec((1, tt, D_GRP), lambda b, t: (b, t, 0)),
                   pl.BlockSpec((1, A_CONV - 1, D_GRP), lambda b, t: (b, 0, 0)),
                   pl.BlockSpec((1, B_CONV - 1, D_GRP), lambda b, t: (b, 0, 0))],
        out_shape=[jax.ShapeDtypeStruct((B, T, D_GRP), BF16),
                   jax.ShapeDtypeStruct((B, T, D_GRP), BF16),
                   jax.ShapeDtypeStruct((B, A_CONV - 1, D_GRP), F32),
                   jax.ShapeDtypeStruct((B, B_CONV - 1, D_GRP), F32)],
        scratch_shapes=[pltpu.VMEM((HA + tt, D_GRP), F32), pltpu.VMEM((8, HB + tt, D_GRP), F32)],
        compiler_params=_cparams("parallel", "arbitrary"),
        name="conv_ab",
    )(proj3, proj3, proj3, proj3, proj3, proj3, proj3, hist_a, hist_b, wa, wb, bias_b, ln_w, ln_b)


def _cumsum_kernel(x_ref, o_ref, carry, *, cb, stride):
    j = pl.program_id(0)

    @pl.when(j == 0)
    def _():
        carry[...] = jnp.zeros_like(carry)

    lane = lax.broadcasted_iota(jnp.int32, carry.shape, 1)
    head = jnp.where(lane < stride, pltpu.roll(carry[...], stride, 1), 0.0)
    x = x_ref[...]
    x = jnp.concatenate([x[:, 0:LANES] + head, x[:, LANES:]], axis=1) if cb > LANES else x + head
    hi, mid, lo = _split3(x)
    r = lax.broadcasted_iota(jnp.int32, (cb, cb), 0)
    c = lax.broadcasted_iota(jnp.int32, (cb, cb), 1)
    same_series = lax.bitwise_and(r, stride - 1) == lax.bitwise_and(c, stride - 1)
    u = ((r <= c) & same_series).astype(BF16)
    cs = _dot(hi, u) + _dot(mid, u) + _dot(lo, u)
    o_ref[...] = cs
    carry[...] = cs[:, cb - LANES:cb]


def cumsum_lanes(x, stride=1):
    R, L = x.shape
    cb = 256 if L % 256 == 0 else LANES
    return pl.pallas_call(
        functools.partial(_cumsum_kernel, cb=cb, stride=stride),
        grid=(L // cb,),
        in_specs=[pl.BlockSpec((R, cb), lambda j: (0, j))],
        out_specs=pl.BlockSpec((R, cb), lambda j: (0, j)),
        out_shape=jax.ShapeDtypeStruct((R, L), F32),
        scratch_shapes=[pltpu.VMEM((R, LANES), F32)],
        compiler_params=_cparams("arbitrary"),
        name="cumsum_lanes",
    )(x)


LOG2E = 1.4426950408889634
ATT_PREP_ROWS = 1024
ATT_SPLIT = 2


def _bias_lanes(d, key_side):
    hi, mid, lo = (p.astype(F32) for p in _split3(d))
    lane = lax.broadcasted_iota(jnp.int32, (d.shape[0], LANES), 1)
    if key_side:
        a = jnp.where(lane == 3, -hi, jnp.where(lane == 4, -mid, jnp.where(lane == 5, -lo, 0.0)))
        a = jnp.where(lane < 3, 1.0, a)
    else:
        a = jnp.where(lane == 0, hi, jnp.where(lane == 1, mid, jnp.where(lane == 2, lo, 0.0)))
        a = jnp.where((lane >= 3) & (lane < 6), 1.0, a)
    return a.astype(BF16)


def _fox_prompt_kernel(q_ref, k_ref, v_ref, z_ref, dq_ref, dk_ref, y_ref,
                       ka_scr, va_scr, qa_scr, s0_scr, s1_scr, p_scr, alpha_scr, m_scr, l_scr, acc_scr, *, tile):
    qi = pl.program_id(1)
    T = k_ref.shape[0]
    th = tile // ATT_SPLIT

    @pl.when(qi == 0)
    def _():
        pr = min(T, ATT_PREP_ROWS)
        one_lane = (lax.broadcasted_iota(jnp.int32, (pr, LANES), 1) == 0).astype(BF16)

        def prep(i, _):
            rows = pl.ds(pl.multiple_of(i * pr, pr), pr)
            ka_scr[rows, 0:DH] = k_ref[rows, :]
            ka_scr[rows, DH:2 * DH] = _bias_lanes(dk_ref[rows, :] * LOG2E, True)
            va_scr[rows, 0:DH] = v_ref[rows, :]
            va_scr[rows, DH:2 * DH] = one_lane
            return 0

        lax.fori_loop(0, T // pr, prep, 0)

    qa_scr[:, 0:DH] = (q_ref[...].astype(F32) * (DH ** -0.5 * LOG2E)).astype(BF16)
    qa_scr[:, DH:2 * DH] = _bias_lanes(dq_ref[...] * LOG2E, False)

    m_scr[...] = jnp.full(m_scr.shape, -jnp.inf, F32)
    l_scr[...] = jnp.zeros(l_scr.shape, F32)
    acc_scr[...] = jnp.zeros(acc_scr.shape, F32)

    def logits(ki, s_buf):
        off = pl.multiple_of(ki * tile, tile)
        ka = ka_scr[pl.ds(off, tile), :]
        for g in range(ATT_SPLIT):
            rows = slice(g * th, (g + 1) * th)
            s_buf[rows, :] = _dot_nt(qa_scr[rows, :], ka)

    def softmax(s_buf, masked):
        for g in range(ATT_SPLIT):
            rows = slice(g * th, (g + 1) * th)
            s = s_buf[rows, :]
            if masked:
                r = lax.broadcasted_iota(jnp.int32, (th, tile), 0) + g * th
                c = lax.broadcasted_iota(jnp.int32, (th, tile), 1)
                s = jnp.where(c <= r, s, -jnp.inf)
            m = m_scr[rows, :]
            m_new = jnp.maximum(m, jnp.max(s, axis=-1, keepdims=True))
            alpha_scr[rows, :] = jnp.exp2(m - m_new)
            p_scr[rows, :] = jnp.exp2(s - m_new).astype(BF16)
            m_scr[rows, :] = m_new

    def weighted_values(ki):
        off = pl.multiple_of(ki * tile, tile)
        va = va_scr[pl.ds(off, tile), :]
        for g in range(ATT_SPLIT):
            rows = slice(g * th, (g + 1) * th)
            pv = _dot(p_scr[rows, :], va)
            alpha = alpha_scr[rows, :]
            l_scr[rows, :] = alpha * l_scr[rows, :] + pv[:, DH:DH + 1]
            acc_scr[rows, :] = alpha * acc_scr[rows, :] + pv[:, 0:DH]

    p_scr[...] = jnp.zeros(p_scr.shape, BF16)
    alpha_scr[...] = jnp.ones(alpha_scr.shape, F32)
    s_bufs = (s0_scr, s1_scr)
    logits(0, s_bufs[0])

    def stage(ki, parity, masked):
        weighted_values(jnp.maximum(ki - 1, 0))
        softmax(s_bufs[parity], masked)
        if not masked:
            logits(ki + 1, s_bufs[1 - parity])

    def pair(j, _):
        stage(2 * j, 0, False)
        stage(2 * j + 1, 1, False)
        return 0

    lax.fori_loop(0, lax.shift_right_logical(qi, 1), pair, 0)
    odd = lax.bitwise_and(qi, 1) == 1

    @pl.when(odd)
    def _():
        stage(qi - 1, 0, False)
        stage(qi, 1, True)

    @pl.when(jnp.logical_not(odd))
    def _():
        stage(qi, 0, True)

    weighted_values(qi)
    y_ref[...] = (_silu(z_ref[...].astype(F32)) * (acc_scr[...] / l_scr[...])).astype(BF16)


def fox_prompt(proj, dcol, *, tile):
    T = proj.shape[0]
    cpb = D_GRP // DH
    return pl.pallas_call(
        functools.partial(_fox_prompt_kernel, tile=tile),
        grid=(H, T // tile),
        in_specs=[pl.BlockSpec((tile, DH), lambda h, i: (i, S_CQ * cpb + h)),
                  pl.BlockSpec((T, DH), lambda h, i: (0, S_CK * cpb + h)),
                  pl.BlockSpec((T, DH), lambda h, i: (0, S_CV * cpb + h)),
                  pl.BlockSpec((tile, DH), lambda h, i: (i, S_CZ * cpb + h)),
                  pl.BlockSpec((None, tile, 1), lambda h, i: (h, i, 0)),
                  pl.BlockSpec((None, T, 1), lambda h, i: (h, 0, 0))],
        out_specs=pl.BlockSpec((tile, DH), lambda h, i: (i, h)),
        out_shape=jax.ShapeDtypeStruct((T, D_GRP), BF16),
        scratch_shapes=[pltpu.VMEM((T, 2 * DH), BF16), pltpu.VMEM((T, 2 * DH), BF16),
                        pltpu.VMEM((tile, 2 * DH), BF16), pltpu.VMEM((tile, tile), F32),
                        pltpu.VMEM((tile, tile), F32), pltpu.VMEM((tile, tile), BF16), pltpu.VMEM((tile, 1), F32),
                        pltpu.VMEM((tile, 1), F32), pltpu.VMEM((tile, 1), F32), pltpu.VMEM((tile, DH), F32)],
        compiler_params=_cparams("parallel", "arbitrary"),
        name="fox_prompt",
    )(proj, proj, proj, proj, dcol, dcol)


def _fox_sample_kernel(q_ref, kn_ref, vn_ref, z_ref, kc_ref, vc_ref, dkh_ref, dq_ref, dkn_ref,
                       y_ref, m_scr, l_scr, acc_scr, *, tq):
    kt = pl.program_id(1)
    nk = pl.num_programs(1)
    scale = DH ** -0.5

    @pl.when(kt == 0)
    def _():
        m_scr[...] = jnp.full(m_scr.shape, -jnp.inf, F32)
        l_scr[...] = jnp.zeros(l_scr.shape, F32)
        acc_scr[...] = jnp.zeros(acc_scr.shape, F32)

    def update(rows, s, v):
        m = m_scr[rows, :]
        m_new = jnp.maximum(m, jnp.max(s, axis=-1, keepdims=True))
        alpha = jnp.exp(m - m_new)
        p = jnp.exp(s - m_new)
        l_scr[rows, :] = alpha * l_scr[rows, :] + jnp.sum(p, axis=-1, keepdims=True)
        acc_scr[rows, :] = alpha * acc_scr[rows, :] + _dot(p.astype(BF16), v)
        m_scr[rows, :] = m_new

    heads = [slice(h * DH, (h + 1) * DH) for h in range(H)]
    q_stack = jnp.concatenate([q_ref[0, :, cs] for cs in heads], axis=0)
    dq_stack = jnp.concatenate([dq_ref[0, :, h:h + 1] for h in range(H)], axis=0)
    kf = kc_ref[0].astype(BF16)
    n = kf.shape[0]
    s = _dot_nt(q_stack, kf) * scale + (dq_stack - dkh_ref[0])
    row_head = lax.shift_right_logical(lax.broadcasted_iota(jnp.int32, (H * tq, n), 0), tq.bit_length() - 1)
    col_head = lax.bitwise_and(lax.broadcasted_iota(jnp.int32, (H * tq, n), 1), H - 1)
    update(slice(None), jnp.where(row_head == col_head, s, -jnp.inf), vc_ref[0].astype(BF16))

    @pl.when(kt == nk - 1)
    def _():
        r = lax.broadcasted_iota(jnp.int32, (tq, tq), 0)
        c = lax.broadcasted_iota(jnp.int32, (tq, tq), 1)
        for h in range(H):
            rows = slice(h * tq, (h + 1) * tq)
            s_new = _dot_nt(q_ref[0, :, heads[h]], kn_ref[0, :, heads[h]]) * scale \
                + (dq_ref[0, :, h:h + 1] - dkn_ref[0, h:h + 1, :])
            update(rows, jnp.where(c <= r, s_new, -jnp.inf), vn_ref[0, :, heads[h]])
            o = acc_scr[rows, :] / l_scr[rows, :]
            y_ref[0, :, heads[h]] = (_silu(z_ref[0, :, heads[h]].astype(F32)) * o).astype(BF16)


def fox_sample(proj3, cache_k, cache_v, layer, dkh, dq_col, dkn_row, *, tk):
    B, tq, _ = proj3.shape
    n = tk * H
    assert tq & (tq - 1) == 0 and H & (H - 1) == 0

    def col(g):
        return pl.BlockSpec((1, tq, D_GRP), lambda b, t, g=g: (b, 0, g))

    cache_spec = pl.BlockSpec((None, 1, n, DH), lambda b, t: (layer, b, t, 0))
    return pl.pallas_call(
        functools.partial(_fox_sample_kernel, tq=tq),
        grid=(B, cache_k.shape[2] // n),
        in_specs=[col(S_CQ), col(S_CK), col(S_CV), col(S_CZ), cache_spec, cache_spec,
                  pl.BlockSpec((None, 1, 1, n), lambda b, t: (layer, b, 0, t)),
                  pl.BlockSpec((1, tq, H), lambda b, t: (b, 0, 0)),
                  pl.BlockSpec((1, H, tq), lambda b, t: (b, 0, 0))],
        out_specs=pl.BlockSpec((1, tq, D_GRP), lambda b, t: (b, 0, 0)),
        out_shape=jax.ShapeDtypeStruct((B, tq, D_GRP), BF16),
        scratch_shapes=[pltpu.VMEM((H * tq, 1), F32), pltpu.VMEM((H * tq, 1), F32),
                        pltpu.VMEM((H * tq, DH), F32)],
        compiler_params=_cparams("parallel", "arbitrary"),
        name="fox_sample",
    )(proj3, proj3, proj3, proj3, cache_k, cache_v, dkh, dq_col, dkn_row)


def _lower_bound(lbl_ref, layer):
    x = lbl_ref[...]
    e = jnp.exp(x - jnp.max(x, axis=0, keepdims=True))
    sm = e / jnp.sum(e, axis=0, keepdims=True)
    cum = sm[0:1, :]
    first = cum
    for i in range(1, layer + 1):
        cum = cum + sm[i:i + 1, :]
    return cum - first


def _gla_levels(C):
    return [C >> i for i in range(1, C.bit_length())]


def _gla_group(q, gl, vi, lb, st_list, b_scr, *, C):
    G = len(st_list)
    N = G * C
    sig = _sigmoid(gl)
    g = jnp.log(lb + (1.0 - lb) * sig)
    kk = (1.0 - lb) * (1.0 - sig)

    row = lax.broadcasted_iota(jnp.int32, (N, N), 0)
    colm = lax.broadcasted_iota(jnp.int32, (N, N), 1)

    def same_block(size):
        sh = size.bit_length() - 1
        return lax.shift_right_logical(row, sh) == lax.shift_right_logical(colm, sh)

    tril = (same_block(C) & (colm <= row)).astype(BF16)
    g_hi, g_mid, g_lo = _split3(g)
    b = _dot(tril, g_hi) + _dot(tril, g_mid) + _dot(tril, g_lo)
    b_scr[...] = b

    def gather_rows(size, offset):
        parts = [jnp.broadcast_to(b_scr[i * size + offset:i * size + offset + 1, :], (size, LANES))
                 for i in range(N // size)]
        return parts[0] if len(parts) == 1 else jnp.concatenate(parts, axis=0)

    trow = lax.broadcasted_iota(jnp.int32, (N, LANES), 0)
    qb16 = q.astype(BF16)
    kb16 = kk.astype(BF16)
    a = jnp.where(row == colm, _dot_nt(qb16, kb16), 0.0)
    for m in _gla_levels(C):
        size = 2 * m
        u = lax.bitwise_and(trow, size - 1)
        upper = u >= m
        if size >= 8:
            ref = gather_rows(size, m - 1)
        elif size == 4:
            ref = jnp.where(u == 0, pltpu.roll(b, N - 1, 0),
                            jnp.where(u == 1, b, jnp.where(u == 2, pltpu.roll(b, 1, 0), pltpu.roll(b, 2, 0))))
        else:
            ref = jnp.where(u == 1, pltpu.roll(b, 1, 0), b)
        x = jnp.exp(jnp.where(upper, b - ref, ref - b))
        qm = jnp.where(upper, q * x, 0.0).astype(BF16)
        km = jnp.where(upper, 0.0, kk * x).astype(BF16)
        a = a + jnp.where(same_block(size), _dot_nt(qm, km), 0.0)

    vb16 = vi.astype(BF16)
    o = _dot(a.astype(BF16), vb16)

    qin = (q * jnp.exp(b)).astype(BF16)
    kd = kk * jnp.exp(gather_rows(C, C - 1) - b)
    vt = vi.T.astype(BF16)
    o_parts, st_new = [], []
    for gi in range(G):
        st = st_list[gi]
        o_parts.append(_dot_nt(qin[gi * C:(gi + 1) * C, :], st.astype(BF16)))
        seg = lax.shift_right_logical(trow, C.bit_length() - 1) == gi
        kd_g = jnp.where(seg, kd, 0.0).astype(BF16)
        ebl = jnp.exp(b_scr[(gi + 1) * C - 1:(gi + 1) * C, :])
        st_new.append(st * ebl + _dot(vt, kd_g))
    o = o + (o_parts[0] if G == 1 else jnp.concatenate(o_parts, axis=0))
    return o, st_new


def _gla_out(o, z, dn):
    ms = jnp.mean(o * o, axis=-1, keepdims=True)
    return (_silu(z) * (o * lax.rsqrt(ms + EPS) * dn)).astype(BF16)


def _hgrn_prompt_kernel(q_ref, f_ref, i_ref, z_ref, lbl_ref, dn_ref, y_ref, s_ref, st_scr, b_scr,
                        *, layer, nseq):
    step = pl.program_id(0)
    C = CHUNK

    @pl.when(step == 0)
    def _():
        st_scr[...] = jnp.zeros(st_scr.shape, F32)

    lb_row = _lower_bound(lbl_ref, layer)
    lb = jnp.concatenate([jnp.broadcast_to(lb_row[:, h * DH:(h + 1) * DH], (C, DH)) for h in range(H)], axis=0)
    dn = dn_ref[...]

    def stack(ref, r0):
        return jnp.concatenate([ref[r0:r0 + C, h * DH:(h + 1) * DH].astype(F32) for h in range(H)], axis=0)

    for cidx in range(nseq):
        r0 = cidx * C
        st_list = [st_scr[h] for h in range(H)]
        o, st_new = _gla_group(stack(q_ref, r0), stack(f_ref, r0), stack(i_ref, r0), lb, st_list, b_scr, C=C)
        y = _gla_out(o, stack(z_ref, r0), dn)
        for h in range(H):
            st_scr[h] = st_new[h]
            y_ref[r0:r0 + C, h * DH:(h + 1) * DH] = y[h * C:(h + 1) * C, :]

    @pl.when(step == pl.num_programs(0) - 1)
    def _():
        for h in range(H):
            s_ref[h] = st_scr[h].T


def hgrn_prompt(proj, lb_logits, d_norm, layer, *, nseq=4):
    T = proj.shape[0]
    rows = nseq * CHUNK

    def col(g):
        return pl.BlockSpec((rows, D_GRP), lambda i, g=g: (i, g))

    return pl.pallas_call(
        functools.partial(_hgrn_prompt_kernel, layer=layer, nseq=nseq),
        grid=(T // rows,),
        in_specs=[col(S_DQ), col(S_DF), col(S_DI), col(S_DZ),
                  pl.BlockSpec(lb_logits.shape, lambda i: (0, 0)),
                  pl.BlockSpec((None, 1, DH), lambda i: (layer, 0, 0))],
        out_specs=[pl.BlockSpec((rows, D_GRP), lambda i: (i, 0)),
                   pl.BlockSpec((H, DH, DH), lambda i: (0, 0, 0))],
        out_shape=[jax.ShapeDtypeStruct((T, D_GRP), BF16),
                   jax.ShapeDtypeStruct((H, DH, DH), F32)],
        scratch_shapes=[pltpu.VMEM((H, DH, DH), F32), pltpu.VMEM((H * CHUNK, DH), F32)],
        compiler_params=_cparams("arbitrary"),
        name="hgrn_prompt",
    )(proj, proj, proj, proj, lb_logits, d_norm)


def _hgrn_sample_kernel(q_ref, f_ref, i_ref, z_ref, s0_ref, lbl_ref, dn_ref, y_ref, s_ref, b_scr,
                        *, layer, nb, C):
    lb_row = _lower_bound(lbl_ref, layer)
    segs = [(bi, h) for bi in range(nb) for h in range(H)]
    lb = jnp.concatenate([jnp.broadcast_to(lb_row[:, h * DH:(h + 1) * DH], (C, DH)) for _, h in segs], axis=0)

    def stack(ref):
        return jnp.concatenate([ref[bi, :, h * DH:(h + 1) * DH].astype(F32) for bi, h in segs], axis=0)

    st_list = [s0_ref[bi, h].T for bi, h in segs]
    o, st_new = _gla_group(stack(q_ref), stack(f_ref), stack(i_ref), lb, st_list, b_scr, C=C)
    y = _gla_out(o, stack(z_ref), dn_ref[...])
    for gi, (bi, h) in enumerate(segs):
        s_ref[bi, h] = st_new[gi].T
        y_ref[bi, :, h * DH:(h + 1) * DH] = y[gi * C:(gi + 1) * C, :]


def hgrn_sample(proj3, state, lb_logits, d_norm, layer, *, nb=4):
    B, C, _ = proj3.shape

    def col(g):
        return pl.BlockSpec((nb, C, D_GRP), lambda i, g=g: (i, 0, g))

    return pl.pallas_call(
        functools.partial(_hgrn_sample_kernel, layer=layer, nb=nb, C=C),
        grid=(B // nb,),
        in_specs=[col(S_DQ), col(S_DF), col(S_DI), col(S_DZ),
                  pl.BlockSpec((None, nb, H, DH, DH), lambda i: (layer, i, 0, 0, 0)),
                  pl.BlockSpec(lb_logits.shape, lambda i: (0, 0)),
                  pl.BlockSpec((None, 1, DH), lambda i: (layer, 0, 0))],
        out_specs=[pl.BlockSpec((nb, C, D_GRP), lambda i: (i, 0, 0)),
                   pl.BlockSpec((nb, H, DH, DH), lambda i: (i, 0, 0, 0))],
        out_shape=[jax.ShapeDtypeStruct((B, C, D_GRP), BF16),
                   jax.ShapeDtypeStruct((B, H, DH, DH), F32)],
        scratch_shapes=[pltpu.VMEM((nb * H * C, DH), F32)],
        compiler_params=_cparams("parallel"),
        name="hgrn_sample",
    )(proj3, proj3, proj3, proj3, state, lb_logits, d_norm)


def _outproj_kernel(ya_ref, yb_ref, yc_ref, yd_ref, w_ref, x_ref, g_ref, fw_ref, o_ref, *, final):
    acc = _dot(ya_ref[...].reshape(-1, D_GRP), w_ref[0:D_GRP, :])
    for i, ref in enumerate((yb_ref, yc_ref, yd_ref), start=1):
        acc = acc + _dot(ref[...].reshape(-1, D_GRP), w_ref[i * D_GRP:(i + 1) * D_GRP, :])
    x = x_ref[...]
    xn = x + g_ref[...] * acc.reshape(x.shape)
    if final:
        ms = jnp.mean(xn * xn, axis=-1, keepdims=True)
        xn = xn * lax.rsqrt(ms + EPS) * fw_ref[...]
    o_ref[...] = xn


def out_proj(ya, yb, yc, yd, layer, w_out, x, mods, final_w, *, bb, tt, final):
    B, T, D = x.shape
    nt = T // tt

    def ymap(i):
        return (i // nt, i % nt, 0)

    yspec = pl.BlockSpec((bb, tt, D_GRP), ymap)
    return pl.pallas_call(
        functools.partial(_outproj_kernel, final=final),
        grid=((B // bb) * nt,),
        in_specs=[yspec, yspec, yspec, yspec,
                  pl.BlockSpec((None,) + w_out.shape[1:], lambda i: (layer, 0, 0)),
                  pl.BlockSpec((bb, tt, D), ymap),
                  pl.BlockSpec((None, None, bb, 1, D), lambda i: (layer, GATE, i // nt, 0, 0)),
                  pl.BlockSpec((1, 1, D), lambda i: (0, 0, 0))],
        out_specs=pl.BlockSpec((bb, tt, D), ymap),
        out_shape=jax.ShapeDtypeStruct((B, T, D), F32),
        compiler_params=_cparams("parallel"),
        name="out_proj",
    )(ya, yb, yc, yd, w_out, x, mods, final_w)


def kernel(x_prompt, x_sample, c_prompt, c_sample, cache_a_conv, cache_b_conv, cache_k, cache_v, cache_logf, state_hgrn, norm_w, w_ada, b_ada, w_in, b_f, a_conv_w, b_conv_w, b_conv_b, b_ln_w, b_ln_b, d_norm_w, hgrn_lb_logits, w_out, final_norm_w):
    L = DEPTH
    Bp, Tp, D = x_prompt.shape
    Bs, Ts, _ = x_sample.shape
    P = cache_k.shape[2]
    assert Bp == 1

    nf = 11 * D_GRP
    w_in_b = w_in.astype(BF16)
    w_main = jnp.concatenate([w_in_b[:, :, :nf], w_in_b[:, :, nf + H:]], axis=-1)
    w_f = jnp.pad(w_in_b[:, :, nf:nf + H], ((0, 0), (0, 0), (0, LANES - H)))
    bf_pad = jnp.pad(b_f, ((0, 0), (0, LANES - H))).reshape(L, 1, LANES)
    w_out_b = w_out.astype(BF16)
    norm_w3 = norm_w.reshape(L, 1, D)
    final_w3 = final_norm_w.reshape(1, 1, D)
    conv_b3, ln_w3, ln_b3 = (a.reshape(L, 1, D_GRP) for a in (b_conv_b, b_ln_w, b_ln_b))
    d_norm3 = d_norm_w.reshape(L, 1, DH)

    nc = Bp + Bs
    c_all = jnp.pad(jnp.concatenate([c_prompt, c_sample], axis=0), ((0, (-nc) % 8), (0, 0)))
    mod = ada_mod(c_all, w_ada, b_ada)
    mod5 = jnp.transpose(mod.reshape(L, -1, 3, 1, D), (0, 2, 1, 3, 4))
    mods_p, mods_s = mod5[:, :, :Bp], mod5[:, :, Bp:nc]

    cache_kf = cache_k.reshape(L, Bs, P * H, DH)
    cache_vf = cache_v.reshape(L, Bs, P * H, DH)
    pad_rows = (-(L * Bs)) % 8
    hist_lf = jnp.pad(cache_logf.reshape(L * Bs, P * H), ((0, pad_rows), (0, 0)))
    dkh_all = cumsum_lanes(hist_lf, stride=H)[:L * Bs].reshape(L, Bs, 1, P * H)
    hist_total = dkh_all[:, :, :, (P - 1) * H:]

    xp, xs = x_prompt, x_sample
    zeros_a = jnp.zeros((1, Bp, A_CONV - 1, D_GRP), F32)
    zeros_b = jnp.zeros((1, Bp, B_CONV - 1, D_GRP), F32)
    outs_p = [[] for _ in range(4)]
    outs_s = [[] for _ in range(4)]
    kv_p = kv_s = None

    for l in range(L):
        last = l == L - 1

        proj, kbuf, vbuf, lf_p = in_proj(xp, l, norm_w3, mods_p, w_main, w_f, bf_pad, kv_p,
                                         bb=1, tt=min(IN_TM, Tp))
        kv_p = (kbuf, vbuf)
        ya, yb, na_p, nb_p = conv_ab(proj.reshape(Bp, Tp, -1), l, zeros_a, zeros_b, 0, a_conv_w, b_conv_w,
                                     conv_b3, ln_w3, ln_b3, tt=CONV_TT)
        lf_t = jnp.pad(lf_p[:, :H].T, ((0, 8 - H), (0, 0)))
        dcum = cumsum_lanes(lf_t)[:H]
        yc = fox_prompt(proj, dcum.reshape(H, Tp, 1), tile=ATT_T)
        yd, s_p = hgrn_prompt(proj, hgrn_lb_logits, d_norm3, l)
        xp = out_proj(ya, yb, yc.reshape(Bp, Tp, D_GRP), yd.reshape(Bp, Tp, D_GRP), l, w_out_b, xp, mods_p,
                      final_w3, bb=1, tt=OUT_TM, final=last)
        for i, a in enumerate((na_p, nb_p, lf_p[:, :H].reshape(Bp, Tp, H), s_p.reshape(Bp, H, DH, DH))):
            outs_p[i].append(a)

        proj, kbuf, vbuf, lf_s = in_proj(xs, l, norm_w3, mods_s, w_main, w_f, bf_pad, kv_s, bb=Bs, tt=Ts)
        kv_s = (kbuf, vbuf)
        proj3 = proj.reshape(Bs, Ts, -1)
        ya, yb, na_s, nb_s = conv_ab(proj3, l, cache_a_conv, cache_b_conv, l, a_conv_w, b_conv_w,
                                     conv_b3, ln_w3, ln_b3, tt=Ts)
        lf_new = lf_s[:, :H].reshape(Bs, Ts, H)
        lf_rows = jnp.pad(lf_new.reshape(Bs, Ts * H), ((0, (-Bs) % 8), (0, LANES - Ts * H)))
        dnew = cumsum_lanes(lf_rows, stride=H)[:Bs, :Ts * H].reshape(Bs, Ts, H) + hist_total[l]
        yc = fox_sample(proj3, cache_kf, cache_vf, l, dkh_all, dnew, jnp.transpose(dnew, (0, 2, 1)),
                        tk=min(ATT_TK_CACHE, P))
        yd, s_s = hgrn_sample(proj3, state_hgrn, hgrn_lb_logits, d_norm3, l)
        xs = out_proj(ya, yb, yc, yd, l, w_out_b, xs, mods_s, final_w3, bb=Bs, tt=Ts, final=last)
        for i, a in enumerate((na_s, nb_s, lf_new, s_s)):
            outs_s[i].append(a)

    stk_p = [jnp.stack(o) for o in outs_p]
    stk_s = [jnp.stack(o) for o in outs_s]
    k_p, v_p = (a.reshape(L, Bp, Tp, H, DH) for a in kv_p)
    k_s, v_s = (a.reshape(L, Bs, Ts, H, DH) for a in kv_s)
    return (xp, xs, stk_p[0], stk_s[0], stk_p[1], stk_s[1], k_p, k_s, v_p, v_s,
            stk_p[2], stk_s[2], stk_p[3], stk_s[3])
```

```python
import functools

import jax
import jax.numpy as jnp
from jax import lax
from jax.experimental import pallas as pl
from jax.experimental.pallas import tpu as pltpu

F32 = jnp.float32
BF16 = jnp.bfloat16

DEPTH = 4
D_MODEL = 2048
D_GRP = 512
N_SPLIT = 15
H = 4
DH = 128
A_CONV = 3
B_CONV = 31
CHUNK = 64
EPS = 1e-6
LANES = 128
VMEM_LIMIT = 56 * 1024 * 1024

(S_AB, S_AC, S_AX, S_AZ, S_BA, S_BG, S_BZ, S_CQ, S_CK, S_CV, S_CZ, S_DQ, S_DF, S_DI, S_DZ) = range(15)

IN_TN = 1536
IN_TM = 1024
CONV_TT = 512
ATT_TK = 512
ATT_TQ = 2048
ATT_TK_CACHE = 4096
OUT_TM = 512


def _cparams(*sem):
    return pltpu.CompilerParams(dimension_semantics=sem, vmem_limit_bytes=VMEM_LIMIT)


def _sigmoid(x):
    return 1.0 / (1.0 + jnp.exp(-x))


def _silu(x):
    return x * _sigmoid(x)


def _log_sigmoid(x):
    return jnp.minimum(x, 0.0) - jnp.log(1.0 + jnp.exp(-jnp.abs(x)))


def _split3(x):
    hi = x.astype(BF16)
    r1 = x - hi.astype(F32)
    mid = r1.astype(BF16)
    lo = (r1 - mid.astype(F32)).astype(BF16)
    return hi, mid, lo


def _dot(a, b):
    return jnp.dot(a, b, preferred_element_type=F32)


def _dot_nt(a, b):
    return lax.dot_general(a, b, (((1,), (1,)), ((), ())), preferred_element_type=F32)


def _ada_kernel(c_ref, w_ref, b_ref, o_ref):
    c = c_ref[...]
    a = _silu(c).astype(BF16)
    o_ref[...] = _dot(a, w_ref[...].astype(BF16)) + b_ref[...]


def ada_mod(c_all, w_ada, b_ada):
    L, D, N = w_ada.shape
    R = c_all.shape[0]
    tn = 768
    return pl.pallas_call(
        _ada_kernel,
        grid=(L, N // tn),
        in_specs=[pl.BlockSpec((R, D), lambda l, j: (0, 0)),
                  pl.BlockSpec((None, D, tn), lambda l, j: (l, 0, j)),
                  pl.BlockSpec((None, 1, tn), lambda l, j: (l, 0, j))],
        out_specs=pl.BlockSpec((None, R, tn), lambda l, j: (l, 0, j)),
        out_shape=jax.ShapeDtypeStruct((L, R, N), F32),
        compiler_params=_cparams("parallel", "parallel"),
        name="ada_mod",
    )(c_all, w_ada, b_ada.reshape(L, 1, N))


def _modulated_norm(x, nw, scale, shift):
    ms = jnp.mean(x * x, axis=-1, keepdims=True)
    h = (x * lax.rsqrt(ms + EPS) * nw) * (1.0 + scale) + shift
    return h.reshape(-1, h.shape[-1]).astype(BF16)


def _inproj_kernel(x_ref, nw_ref, sc_ref, sh_ref, w_ref, wf_ref, bf_ref, *rest):
    proj_ref, k_ref, v_ref, logf_ref, h_scr = rest[-5:]
    j = pl.program_id(1)

    @pl.when(j == 0)
    def _():
        hb = _modulated_norm(x_ref[...], nw_ref[...], sc_ref[...], sh_ref[...])
        h_scr[...] = hb
        logf_ref[...] = _log_sigmoid(_dot(hb, wf_ref[...]) + bf_ref[...])

    h = h_scr[...]
    for c in range(IN_TN // D_GRP):
        r = _dot(h, w_ref[:, c * D_GRP:(c + 1) * D_GRP])
        proj_ref[:, c * D_GRP:(c + 1) * D_GRP] = r.astype(BF16)
        for split, ref in ((S_CK, k_ref), (S_CV, v_ref)):
            if split % 3 == c:
                @pl.when(j == split // 3)
                def _(r=r, ref=ref):
                    ref[...] = r


SHIFT, SCALE, GATE = range(3)


def in_proj(x, layer, norm_w, mods, w_main, w_f, b_f, kv_bufs, *, bb, tt):
    B, T, D = x.shape
    L, _, N = w_main.shape
    rows = bb * tt
    nb, nt = B // bb, T // tt
    R = B * T

    def xmap(i, j):
        return (i // nt, i % nt, 0)

    def mod_spec(which):
        return pl.BlockSpec((None, None, bb, 1, D), lambda i, j: (layer, which, i // nt, 0, 0))

    kv_spec = pl.BlockSpec((None, rows, D_GRP), lambda i, j: (layer, i, 0))
    aliased = [] if kv_bufs is None else list(kv_bufs)
    n_in = 7
    return pl.pallas_call(
        _inproj_kernel,
        grid=(nb * nt, N // IN_TN),
        in_specs=[pl.BlockSpec((bb, tt, D), xmap),
                  pl.BlockSpec((1, 1, D), lambda i, j: (layer, 0, 0)),
                  mod_spec(SCALE), mod_spec(SHIFT),
                  pl.BlockSpec((None, D, IN_TN), lambda i, j: (layer, 0, j)),
                  pl.BlockSpec((None, D, LANES), lambda i, j: (layer, 0, 0)),
                  pl.BlockSpec((None, 1, LANES), lambda i, j: (layer, 0, 0))]
                 + [pl.BlockSpec(memory_space=pl.ANY)] * len(aliased),
        out_specs=[pl.BlockSpec((rows, IN_TN), lambda i, j: (i, j)), kv_spec, kv_spec,
                   pl.BlockSpec((rows, LANES), lambda i, j: (i, 0))],
        out_shape=[jax.ShapeDtypeStruct((R, N), BF16),
                   jax.ShapeDtypeStruct((L, R, D_GRP), F32),
                   jax.ShapeDtypeStruct((L, R, D_GRP), F32),
                   jax.ShapeDtypeStruct((R, LANES), F32)],
        input_output_aliases={n_in + a: 1 + a for a in range(len(aliased))},
        scratch_shapes=[pltpu.VMEM((rows, D), BF16)],
        compiler_params=_cparams("parallel", "arbitrary"),
        name="in_proj",
    )(x, norm_w, mods, mods, w_main, w_f, b_f, *aliased)


HA = 8
HB = 32

def _conv_kernel(ab_ref, ac_ref, ax_ref, az_ref, ba_ref, bg_ref, bz_ref, ha_ref, hb_ref,
                 wa_ref, wb_ref, bb_ref, lw_ref, lb_ref,
                 ya_ref, yb_ref, na_ref, nb_ref, seqa, seqb, *, tt):
    t = pl.program_id(1)
    na, nb = A_CONV - 1, B_CONV - 1

    @pl.when(t == 0)
    def _():
        seqa[HA - na:HA, :] = ha_ref[0]
        seqb[0, 0:HB - nb, :] = jnp.zeros((HB - nb, D_GRP), F32)
        seqb[0, HB - nb:HB, :] = hb_ref[0]

    @pl.when(t > 0)
    def _():
        ta = seqa[HA + tt - na:HA + tt, :]
        seqa[HA - na:HA, :] = ta
        tb = seqb[0, HB + tt - nb:HB + tt, :]
        seqb[0, HB - nb:HB, :] = tb

    seqa[HA:HA + tt, :] = ac_ref[0].astype(F32) * ax_ref[0].astype(F32)
    seqb[0, HB:HB + tt, :] = ba_ref[0].astype(F32) * _sigmoid(bg_ref[0].astype(F32))
    for r in range(1, 8):
        seqb[r, 0:HB + tt - r, :] = seqb[0, r:HB + tt, :]

    rc = min(tt, 32)
    for r0 in range(0, tt, rc):
        acc = wa_ref[0:1, :] * seqa[HA - na + r0:HA - na + r0 + rc, :]
        for w in range(1, A_CONV):
            acc = acc + wa_ref[w:w + 1, :] * seqa[HA - na + r0 + w:HA - na + r0 + w + rc, :]
        ya = _silu(az_ref[0, r0:r0 + rc, :].astype(F32)) * ab_ref[0, r0:r0 + rc, :].astype(F32) * acc
        ya_ref[0, r0:r0 + rc, :] = ya.astype(BF16)

        acc = None
        for w in range(B_CONV):
            a8, r8 = divmod(HB - nb + w, 8)
            term = wb_ref[w:w + 1, :] * seqb[r8, 8 * a8 + r0:8 * a8 + r0 + rc, :]
            acc = term if acc is None else acc + term
        y = acc + bb_ref[...]
        mu = jnp.mean(y, axis=-1, keepdims=True)
        yc = y - mu
        var = jnp.mean(yc * yc, axis=-1, keepdims=True)
        yn = yc * lax.rsqrt(var + EPS) * lw_ref[...] + lb_ref[...]
        yb = _silu(bz_ref[0, r0:r0 + rc, :].astype(F32)) * _silu(yn)
        yb_ref[0, r0:r0 + rc, :] = yb.astype(BF16)

    na_ref[0] = seqa[HA + tt - na:HA + tt, :]
    nb_ref[0] = seqb[0, HB + tt - nb:HB + tt, :]


def conv_ab(proj3, layer, hist_a, hist_b, hist_layer, wa, wb, bias_b, ln_w, ln_b, *, tt):
    B, T, _ = proj3.shape
    nt = T // tt

    def col(g):
        return pl.BlockSpec((1, tt, D_GRP), lambda b, t, g=g: (b, t, g))

    def per_layer(a):
        return pl.BlockSpec((None,) + a.shape[1:], lambda b, t: (layer,) + (0,) * (a.ndim - 1))

    return pl.pallas_call(
        functools.partial(_conv_kernel, tt=tt),
        grid=(B, nt),
        in_specs=[col(S_AB), col(S_AC), col(S_AX), col(S_AZ), col(S_BA), col(S_BG), col(S_BZ),
                  pl.BlockSpec((None, 1, A_CONV - 1, D_GRP), lambda b, t: (hist_layer, b, 0, 0)),
                  pl.BlockSpec((None, 1, B_CONV - 1, D_GRP), lambda b, t: (hist_layer, b, 0, 0)),
                  per_layer(wa), per_layer(wb), per_layer(bias_b), per_layer(ln_w), per_layer(ln_b)],
        out_specs=[pl.BlockSpec((1, tt, D_GRP), lambda b, t: (b, t, 0)),
                   pl.BlockSpec((1, tt, D_GRP), lambda b, t: (b, t, 0)),
                   pl.BlockSpec((1, A_CONV - 1, D_GRP), lambda b, t: (b, 0, 0)),
                   pl.BlockSpec((1, B_CONV - 1, D_GRP), lambda b, t: (b, 0, 0))],
        out_shape=[jax.ShapeDtypeStruct((B, T, D_GRP), BF16),
                   jax.ShapeDtypeStruct((B, T, D_GRP), BF16),
                   jax.ShapeDtypeStruct((B, A_CONV - 1, D_GRP), F32),
                   jax.ShapeDtypeStruct((B, B_CONV - 1, D_GRP), F32)],
        scratch_shapes=[pltpu.VMEM((HA + tt, D_GRP), F32), pltpu.VMEM((8, HB + tt, D_GRP), F32)],
        compiler_params=_cparams("parallel", "arbitrary"),
        name="conv_ab",
    )(proj3, proj3, proj3, proj3, proj3, proj3, proj3, hist_a, hist_b, wa, wb, bias_b, ln_w, ln_b)


def _cumsum_kernel(x_ref, o_ref, carry, *, cb, stride):
    j = pl.program_id(0)

    @pl.when(j == 0)
    def _():
        carry[...] = jnp.zeros_like(carry)

    lane = lax.broadcasted_iota(jnp.int32, carry.shape, 1)
    head = jnp.where(lane < stride, pltpu.roll(carry[...], stride, 1), 0.0)
    x = x_ref[...]
    x = jnp.concatenate([x[:, 0:LANES] + head, x[:, LANES:]], axis=1) if cb > LANES else x + head
    hi, mid, lo = _split3(x)
    r = lax.broadcasted_iota(jnp.int32, (cb, cb), 0)
    c = lax.broadcasted_iota(jnp.int32, (cb, cb), 1)
    same_series = lax.bitwise_and(r, stride - 1) == lax.bitwise_and(c, stride - 1)
    u = ((r <= c) & same_series).astype(BF16)
    cs = _dot(hi, u) + _dot(mid, u) + _dot(lo, u)
    o_ref[...] = cs
    carry[...] = cs[:, cb - LANES:cb]


def cumsum_lanes(x, stride=1):
    R, L = x.shape
    cb = 256 if L % 256 == 0 else LANES
    return pl.pallas_call(
        functools.partial(_cumsum_kernel, cb=cb, stride=stride),
        grid=(L // cb,),
        in_specs=[pl.BlockSpec((R, cb), lambda j: (0, j))],
        out_specs=pl.BlockSpec((R, cb), lambda j: (0, j)),
        out_shape=jax.ShapeDtypeStruct((R, L), F32),
        scratch_shapes=[pltpu.VMEM((R, LANES), F32)],
        compiler_params=_cparams("arbitrary"),
        name="cumsum_lanes",
    )(x)


LOG2E = 1.4426950408889634
ATT_PREP_ROWS = 1024
ATT_GROUP_ROWS = 256


def _bias_lanes(d, key_side):
    hi, mid, lo = (p.astype(F32) for p in _split3(d))
    lane = lax.broadcasted_iota(jnp.int32, (d.shape[0], LANES), 1)
    if key_side:
        a = jnp.where(lane == 3, -hi, jnp.where(lane == 4, -mid, jnp.where(lane == 5, -lo, 0.0)))
        a = jnp.where(lane < 3, 1.0, a)
    else:
        a = jnp.where(lane == 0, hi, jnp.where(lane == 1, mid, jnp.where(lane == 2, lo, 0.0)))
        a = jnp.where((lane >= 3) & (lane < 6), 1.0, a)
    return a.astype(BF16)


def _fox_prompt_kernel(q_ref, k_ref, v_ref, z_ref, dq_ref, dk_ref, y_ref,
                       ka_scr, va_scr, qa_scr, s0_scr, s1_scr, p_scr, alpha_scr, m_scr, l_scr, acc_scr, *, tk):
    qi = pl.program_id(1)
    T = k_ref.shape[0]
    th = ATT_GROUP_ROWS
    n_groups = q_ref.shape[0] // th

    @pl.when(qi == 0)
    def _():
        pr = min(T, ATT_PREP_ROWS)
        one_lane = (lax.broadcasted_iota(jnp.int32, (pr, LANES), 1) == 0).astype(BF16)

        def prep(i, _):
            rows = pl.ds(pl.multiple_of(i * pr, pr), pr)
            ka_scr[rows, 0:DH] = k_ref[rows, :]
            ka_scr[rows, DH:2 * DH] = _bias_lanes(dk_ref[rows, :] * LOG2E, True)
            va_scr[rows, 0:DH] = v_ref[rows, :]
            va_scr[rows, DH:2 * DH] = one_lane
            return 0

        lax.fori_loop(0, T // pr, prep, 0)

    qa_scr[:, 0:DH] = (q_ref[...].astype(F32) * (DH ** -0.5 * LOG2E)).astype(BF16)
    qa_scr[:, DH:2 * DH] = _bias_lanes(dq_ref[...] * LOG2E, False)

    m_scr[...] = jnp.full(m_scr.shape, -jnp.inf, F32)
    l_scr[...] = jnp.zeros(l_scr.shape, F32)
    acc_scr[...] = jnp.zeros(acc_scr.shape, F32)

    all_groups = tuple(range(n_groups))

    def logits(ki, s_buf, groups=all_groups):
        off = pl.multiple_of(ki * tk, tk)
        ka = ka_scr[pl.ds(off, tk), :]
        for g in groups:
            rows = slice(g * th, (g + 1) * th)
            s_buf[rows, :] = _dot_nt(qa_scr[rows, :], ka)

    def softmax(s_buf, key_offset=None, groups=all_groups):
        for g in groups:
            rows = slice(g * th, (g + 1) * th)
            s = s_buf[rows, :]
            if key_offset is not None:
                r = lax.broadcasted_iota(jnp.int32, (th, tk), 0) + g * th
                c = lax.broadcasted_iota(jnp.int32, (th, tk), 1) + key_offset
                s = jnp.where(c <= r, s, -jnp.inf)
            m = m_scr[rows, :]
            m_new = jnp.maximum(m, jnp.max(s, axis=-1, keepdims=True))
            alpha_scr[rows, :] = jnp.exp2(m - m_new)
            p_scr[rows, :] = jnp.exp2(s - m_new).astype(BF16)
            m_scr[rows, :] = m_new

    def weighted_values(ki, groups=all_groups):
        off = pl.multiple_of(ki * tk, tk)
        va = va_scr[pl.ds(off, tk), :]
        for g in groups:
            rows = slice(g * th, (g + 1) * th)
            pv = _dot(p_scr[rows, :], va)
            alpha = alpha_scr[rows, :]
            l_scr[rows, :] = alpha * l_scr[rows, :] + pv[:, DH:DH + 1]
            acc_scr[rows, :] = alpha * acc_scr[rows, :] + pv[:, 0:DH]

    p_scr[...] = jnp.zeros(p_scr.shape, BF16)
    alpha_scr[...] = jnp.ones(alpha_scr.shape, F32)
    logits(0, s0_scr)

    def pair(j, _):
        weighted_values(jnp.maximum(2 * j - 1, 0))
        softmax(s0_scr)
        logits(2 * j + 1, s1_scr)
        weighted_values(2 * j)
        softmax(s1_scr)
        logits(2 * j + 2, s0_scr)
        return 0

    n_diag = q_ref.shape[0] // tk
    first = n_diag * qi
    assert n_diag % 2 == 0
    lax.fori_loop(0, qi * (n_diag // 2), pair, 0)
    seeing = [tuple(g for g in all_groups if (g + 1) * th > d * tk) for d in range(n_diag)]
    s_bufs = (s0_scr, s1_scr)
    for d in range(n_diag):
        weighted_values(jnp.maximum(first + d - 1, 0), seeing[d - 1] if d else all_groups)
        softmax(s_bufs[d % 2], key_offset=d * tk, groups=seeing[d])
        if d + 1 < n_diag:
            logits(first + d + 1, s_bufs[(d + 1) % 2], seeing[d + 1])
    weighted_values(first + n_diag - 1, seeing[-1])
    y_ref[...] = (_silu(z_ref[...].astype(F32)) * (acc_scr[...] / l_scr[...])).astype(BF16)


def fox_prompt(proj, dcol, *, tk, tq):
    T = proj.shape[0]
    cpb = D_GRP // DH
    return pl.pallas_call(
        functools.partial(_fox_prompt_kernel, tk=tk),
        grid=(H, T // tq),
        in_specs=[pl.BlockSpec((tq, DH), lambda h, i: (i, S_CQ * cpb + h)),
                  pl.BlockSpec((T, DH), lambda h, i: (0, S_CK * cpb + h)),
                  pl.BlockSpec((T, DH), lambda h, i: (0, S_CV * cpb + h)),
                  pl.BlockSpec((tq, DH), lambda h, i: (i, S_CZ * cpb + h)),
                  pl.BlockSpec((None, tq, 1), lambda h, i: (h, i, 0)),
                  pl.BlockSpec((None, T, 1), lambda h, i: (h, 0, 0))],
        out_specs=pl.BlockSpec((tq, DH), lambda h, i: (i, h)),
        out_shape=jax.ShapeDtypeStruct((T, D_GRP), BF16),
        scratch_shapes=[pltpu.VMEM((T, 2 * DH), BF16), pltpu.VMEM((T, 2 * DH), BF16),
                        pltpu.VMEM((tq, 2 * DH), BF16), pltpu.VMEM((tq, tk), F32),
                        pltpu.VMEM((tq, tk), F32), pltpu.VMEM((tq, tk), BF16), pltpu.VMEM((tq, 1), F32),
                        pltpu.VMEM((tq, 1), F32), pltpu.VMEM((tq, 1), F32), pltpu.VMEM((tq, DH), F32)],
        compiler_params=_cparams("parallel", "arbitrary"),
        name="fox_prompt",
    )(proj, proj, proj, proj, dcol, dcol)


def _fox_sample_kernel(q_ref, kn_ref, vn_ref, z_ref, kc_ref, vc_ref, dkh_ref, dq_ref, dkn_ref,
                       y_ref, m_scr, l_scr, acc_scr, *, tq):
    kt = pl.program_id(1)
    nk = pl.num_programs(1)
    scale = DH ** -0.5

    @pl.when(kt == 0)
    def _():
        m_scr[...] = jnp.full(m_scr.shape, -jnp.inf, F32)
        l_scr[...] = jnp.zeros(l_scr.shape, F32)
        acc_scr[...] = jnp.zeros(acc_scr.shape, F32)

    def update(rows, s, v):
        m = m_scr[rows, :]
        m_new = jnp.maximum(m, jnp.max(s, axis=-1, keepdims=True))
        alpha = jnp.exp(m - m_new)
        p = jnp.exp(s - m_new)
        l_scr[rows, :] = alpha * l_scr[rows, :] + jnp.sum(p, axis=-1, keepdims=True)
        acc_scr[rows, :] = alpha * acc_scr[rows, :] + _dot(p.astype(BF16), v)
        m_scr[rows, :] = m_new

    heads = [slice(h * DH, (h + 1) * DH) for h in range(H)]
    q_stack = jnp.concatenate([q_ref[0, :, cs] for cs in heads], axis=0)
    dq_stack = jnp.concatenate([dq_ref[0, :, h:h + 1] for h in range(H)], axis=0)
    kf = kc_ref[0].astype(BF16)
    n = kf.shape[0]
    s = _dot_nt(q_stack, kf) * scale + (dq_stack - dkh_ref[0])
    row_head = lax.shift_right_logical(lax.broadcasted_iota(jnp.int32, (H * tq, n), 0), tq.bit_length() - 1)
    col_head = lax.bitwise_and(lax.broadcasted_iota(jnp.int32, (H * tq, n), 1), H - 1)
    update(slice(None), jnp.where(row_head == col_head, s, -jnp.inf), vc_ref[0].astype(BF16))

    @pl.when(kt == nk - 1)
    def _():
        r = lax.broadcasted_iota(jnp.int32, (tq, tq), 0)
        c = lax.broadcasted_iota(jnp.int32, (tq, tq), 1)
        for h in range(H):
            rows = slice(h * tq, (h + 1) * tq)
            s_new = _dot_nt(q_ref[0, :, heads[h]], kn_ref[0, :, heads[h]]) * scale \
                + (dq_ref[0, :, h:h + 1] - dkn_ref[0, h:h + 1, :])
            update(rows, jnp.where(c <= r, s_new, -jnp.inf), vn_ref[0, :, heads[h]])
            o = acc_scr[rows, :] / l_scr[rows, :]
            y_ref[0, :, heads[h]] = (_silu(z_ref[0, :, heads[h]].astype(F32)) * o).astype(BF16)


def fox_sample(proj3, cache_k, cache_v, layer, dkh, dq_col, dkn_row, *, tk):
    B, tq, _ = proj3.shape
    n = tk * H
    assert tq & (tq - 1) == 0 and H & (H - 1) == 0

    def col(g):
        return pl.BlockSpec((1, tq, D_GRP), lambda b, t, g=g: (b, 0, g))

    cache_spec = pl.BlockSpec((None, 1, n, DH), lambda b, t: (layer, b, t, 0))
    return pl.pallas_call(
        functools.partial(_fox_sample_kernel, tq=tq),
        grid=(B, cache_k.shape[2] // n),
        in_specs=[col(S_CQ), col(S_CK), col(S_CV), col(S_CZ), cache_spec, cache_spec,
                  pl.BlockSpec((None, 1, 1, n), lambda b, t: (layer, b, 0, t)),
                  pl.BlockSpec((1, tq, H), lambda b, t: (b, 0, 0)),
                  pl.BlockSpec((1, H, tq), lambda b, t: (b, 0, 0))],
        out_specs=pl.BlockSpec((1, tq, D_GRP), lambda b, t: (b, 0, 0)),
        out_shape=jax.ShapeDtypeStruct((B, tq, D_GRP), BF16),
        scratch_shapes=[pltpu.VMEM((H * tq, 1), F32), pltpu.VMEM((H * tq, 1), F32),
                        pltpu.VMEM((H * tq, DH), F32)],
        compiler_params=_cparams("parallel", "arbitrary"),
        name="fox_sample",
    )(proj3, proj3, proj3, proj3, cache_k, cache_v, dkh, dq_col, dkn_row)


def _lower_bound(lbl_ref, layer):
    x = lbl_ref[...]
    e = jnp.exp(x - jnp.max(x, axis=0, keepdims=True))
    sm = e / jnp.sum(e, axis=0, keepdims=True)
    cum = sm[0:1, :]
    first = cum
    for i in range(1, layer + 1):
        cum = cum + sm[i:i + 1, :]
    return cum - first


def _gla_levels(C):
    return [C >> i for i in range(1, C.bit_length())]


def _gla_group(q, gl, vi, lb, st_list, b_scr, *, C):
    G = len(st_list)
    N = G * C
    sig = _sigmoid(gl)
    g = jnp.log(lb + (1.0 - lb) * sig)
    kk = (1.0 - lb) * (1.0 - sig)

    row = lax.broadcasted_iota(jnp.int32, (N, N), 0)
    colm = lax.broadcasted_iota(jnp.int32, (N, N), 1)

    def same_block(size):
        sh = size.bit_length() - 1
        return lax.shift_right_logical(row, sh) == lax.shift_right_logical(colm, sh)

    tril = (same_block(C) & (colm <= row)).astype(BF16)
    g_hi, g_mid, g_lo = _split3(g)
    b = _dot(tril, g_hi) + _dot(tril, g_mid) + _dot(tril, g_lo)
    b_scr[...] = b

    def gather_rows(size, offset):
        parts = [jnp.broadcast_to(b_scr[i * size + offset:i * size + offset + 1, :], (size, LANES))
                 for i in range(N // size)]
        return parts[0] if len(parts) == 1 else jnp.concatenate(parts, axis=0)

    trow = lax.broadcasted_iota(jnp.int32, (N, LANES), 0)
    qb16 = q.astype(BF16)
    kb16 = kk.astype(BF16)
    a = jnp.where(row == colm, _dot_nt(qb16, kb16), 0.0)
    for m in _gla_levels(C):
        size = 2 * m
        u = lax.bitwise_and(trow, size - 1)
        upper = u >= m
        if size >= 8:
            ref = gather_rows(size, m - 1)
        elif size == 4:
            ref = jnp.where(u == 0, pltpu.roll(b, N - 1, 0),
                            jnp.where(u == 1, b, jnp.where(u == 2, pltpu.roll(b, 1, 0), pltpu.roll(b, 2, 0))))
        else:
            ref = jnp.where(u == 1, pltpu.roll(b, 1, 0), b)
        x = jnp.exp(jnp.where(upper, b - ref, ref - b))
        qm = jnp.where(upper, q * x, 0.0).astype(BF16)
        km = jnp.where(upper, 0.0, kk * x).astype(BF16)
        a = a + jnp.where(same_block(size), _dot_nt(qm, km), 0.0)

    vb16 = vi.astype(BF16)
    o = _dot(a.astype(BF16), vb16)

    qin = (q * jnp.exp(b)).astype(BF16)
    kd = kk * jnp.exp(gather_rows(C, C - 1) - b)
    vt = vi.T.astype(BF16)
    o_parts, st_new = [], []
    for gi in range(G):
        st = st_list[gi]
        o_parts.append(_dot_nt(qin[gi * C:(gi + 1) * C, :], st.astype(BF16)))
        seg = lax.shift_right_logical(trow, C.bit_length() - 1) == gi
        kd_g = jnp.where(seg, kd, 0.0).astype(BF16)
        ebl = jnp.exp(b_scr[(gi + 1) * C - 1:(gi + 1) * C, :])
        st_new.append(st * ebl + _dot(vt, kd_g))
    o = o + (o_parts[0] if G == 1 else jnp.concatenate(o_parts, axis=0))
    return o, st_new


def _gla_out(o, z, dn):
    ms = jnp.mean(o * o, axis=-1, keepdims=True)
    return (_silu(z) * (o * lax.rsqrt(ms + EPS) * dn)).astype(BF16)


def _hgrn_prompt_kernel(q_ref, f_ref, i_ref, z_ref, lbl_ref, dn_ref, y_ref, s_ref, st_scr, b_scr,
                        *, layer, nseq):
    step = pl.program_id(0)
    C = CHUNK

    @pl.when(step == 0)
    def _():
        st_scr[...] = jnp.zeros(st_scr.shape, F32)

    lb_row = _lower_bound(lbl_ref, layer)
    lb = jnp.concatenate([jnp.broadcast_to(lb_row[:, h * DH:(h + 1) * DH], (C, DH)) for h in range(H)], axis=0)
    dn = dn_ref[...]

    def stack(ref, r0):
        return jnp.concatenate([ref[r0:r0 + C, h * DH:(h + 1) * DH].astype(F32) for h in range(H)], axis=0)

    for cidx in range(nseq):
        r0 = cidx * C
        st_list = [st_scr[h] for h in range(H)]
        o, st_new = _gla_group(stack(q_ref, r0), stack(f_ref, r0), stack(i_ref, r0), lb, st_list, b_scr, C=C)
        y = _gla_out(o, stack(z_ref, r0), dn)
        for h in range(H):
            st_scr[h] = st_new[h]
            y_ref[r0:r0 + C, h * DH:(h + 1) * DH] = y[h * C:(h + 1) * C, :]

    @pl.when(step == pl.num_programs(0) - 1)
    def _():
        for h in range(H):
            s_ref[h] = st_scr[h].T


def hgrn_prompt(proj, lb_logits, d_norm, layer, *, nseq=8):
    T = proj.shape[0]
    rows = nseq * CHUNK

    def col(g):
        return pl.BlockSpec((rows, D_GRP), lambda i, g=g: (i, g))

    return pl.pallas_call(
        functools.partial(_hgrn_prompt_kernel, layer=layer, nseq=nseq),
        grid=(T // rows,),
        in_specs=[col(S_DQ), col(S_DF), col(S_DI), col(S_DZ),
                  pl.BlockSpec(lb_logits.shape, lambda i: (0, 0)),
                  pl.BlockSpec((None, 1, DH), lambda i: (layer, 0, 0))],
        out_specs=[pl.BlockSpec((rows, D_GRP), lambda i: (i, 0)),
                   pl.BlockSpec((H, DH, DH), lambda i: (0, 0, 0))],
        out_shape=[jax.ShapeDtypeStruct((T, D_GRP), BF16),
                   jax.ShapeDtypeStruct((H, DH, DH), F32)],
        scratch_shapes=[pltpu.VMEM((H, DH, DH), F32), pltpu.VMEM((H * CHUNK, DH), F32)],
        compiler_params=_cparams("arbitrary"),
        name="hgrn_prompt",
    )(proj, proj, proj, proj, lb_logits, d_norm)


def _hgrn_sample_kernel(q_ref, f_ref, i_ref, z_ref, s0_ref, lbl_ref, dn_ref, y_ref, s_ref, b_scr,
                        *, layer, nb, C):
    lb_row = _lower_bound(lbl_ref, layer)
    segs = [(bi, h) for bi in range(nb) for h in range(H)]
    lb = jnp.concatenate([jnp.broadcast_to(lb_row[:, h * DH:(h + 1) * DH], (C, DH)) for _, h in segs], axis=0)

    def stack(ref):
        return jnp.concatenate([ref[bi, :, h * DH:(h + 1) * DH].astype(F32) for bi, h in segs], axis=0)

    st_list = [s0_ref[bi, h].T for bi, h in segs]
    o, st_new = _gla_group(stack(q_ref), stack(f_ref), stack(i_ref), lb, st_list, b_scr, C=C)
    y = _gla_out(o, stack(z_ref), dn_ref[...])
    for gi, (bi, h) in enumerate(segs):
        s_ref[bi, h] = st_new[gi].T
        y_ref[bi, :, h * DH:(h + 1) * DH] = y[gi * C:(gi + 1) * C, :]


def hgrn_sample(proj3, state, lb_logits, d_norm, layer, *, nb=4):
    B, C, _ = proj3.shape

    def col(g):
        return pl.BlockSpec((nb, C, D_GRP), lambda i, g=g: (i, 0, g))

    return pl.pallas_call(
        functools.partial(_hgrn_sample_kernel, layer=layer, nb=nb, C=C),
        grid=(B // nb,),
        in_specs=[col(S_DQ), col(S_DF), col(S_DI), col(S_DZ),
                  pl.BlockSpec((None, nb, H, DH, DH), lambda i: (layer, i, 0, 0, 0)),
                  pl.BlockSpec(lb_logits.shape, lambda i: (0, 0)),
                  pl.BlockSpec((None, 1, DH), lambda i: (layer, 0, 0))],
        out_specs=[pl.BlockSpec((nb, C, D_GRP), lambda i: (i, 0, 0)),
                   pl.BlockSpec((nb, H, DH, DH), lambda i: (i, 0, 0, 0))],
        out_shape=[jax.ShapeDtypeStruct((B, C, D_GRP), BF16),
                   jax.ShapeDtypeStruct((B, H, DH, DH), F32)],
        scratch_shapes=[pltpu.VMEM((nb * H * C, DH), F32)],
        compiler_params=_cparams("parallel"),
        name="hgrn_sample",
    )(proj3, proj3, proj3, proj3, state, lb_logits, d_norm)


def _outproj_kernel(ya_ref, yb_ref, yc_ref, yd_ref, w_ref, x_ref, g_ref, fw_ref, o_ref, *, final):
    acc = _dot(ya_ref[...].reshape(-1, D_GRP), w_ref[0:D_GRP, :])
    for i, ref in enumerate((yb_ref, yc_ref, yd_ref), start=1):
        acc = acc + _dot(ref[...].reshape(-1, D_GRP), w_ref[i * D_GRP:(i + 1) * D_GRP, :])
    x = x_ref[...]
    xn = x + g_ref[...] * acc.reshape(x.shape)
    if final:
        ms = jnp.mean(xn * xn, axis=-1, keepdims=True)
        xn = xn * lax.rsqrt(ms + EPS) * fw_ref[...]
    o_ref[...] = xn


def out_proj(ya, yb, yc, yd, layer, w_out, x, mods, final_w, *, bb, tt, final):
    B, T, D = x.shape
    nt = T // tt

    def ymap(i):
        return (i // nt, i % nt, 0)

    yspec = pl.BlockSpec((bb, tt, D_GRP), ymap)
    return pl.pallas_call(
        functools.partial(_outproj_kernel, final=final),
        grid=((B // bb) * nt,),
        in_specs=[yspec, yspec, yspec, yspec,
                  pl.BlockSpec((None,) + w_out.shape[1:], lambda i: (layer, 0, 0)),
                  pl.BlockSpec((bb, tt, D), ymap),
                  pl.BlockSpec((None, None, bb, 1, D), lambda i: (layer, GATE, i // nt, 0, 0)),
                  pl.BlockSpec((1, 1, D), lambda i: (0, 0, 0))],
        out_specs=pl.BlockSpec((bb, tt, D), ymap),
        out_shape=jax.ShapeDtypeStruct((B, T, D), F32),
        compiler_params=_cparams("parallel"),
        name="out_proj",
    )(ya, yb, yc, yd, w_out, x, mods, final_w)


def kernel(x_prompt, x_sample, c_prompt, c_sample, cache_a_conv, cache_b_conv, cache_k, cache_v, cache_logf, state_hgrn, norm_w, w_ada, b_ada, w_in, b_f, a_conv_w, b_conv_w, b_conv_b, b_ln_w, b_ln_b, d_norm_w, hgrn_lb_logits, w_out, final_norm_w):
    L = DEPTH
    Bp, Tp, D = x_prompt.shape
    Bs, Ts, _ = x_sample.shape
    P = cache_k.shape[2]
    assert Bp == 1

    nf = 11 * D_GRP
    w_in_b = w_in.astype(BF16)
    w_main = jnp.concatenate([w_in_b[:, :, :nf], w_in_b[:, :, nf + H:]], axis=-1)
    w_f = jnp.pad(w_in_b[:, :, nf:nf + H], ((0, 0), (0, 0), (0, LANES - H)))
    bf_pad = jnp.pad(b_f, ((0, 0), (0, LANES - H))).reshape(L, 1, LANES)
    w_out_b = w_out.astype(BF16)
    norm_w3 = norm_w.reshape(L, 1, D)
    final_w3 = final_norm_w.reshape(1, 1, D)
    conv_b3, ln_w3, ln_b3 = (a.reshape(L, 1, D_GRP) for a in (b_conv_b, b_ln_w, b_ln_b))
    d_norm3 = d_norm_w.reshape(L, 1, DH)

    nc = Bp + Bs
    c_all = jnp.pad(jnp.concatenate([c_prompt, c_sample], axis=0), ((0, (-nc) % 8), (0, 0)))
    mod = ada_mod(c_all, w_ada, b_ada)
    mod5 = jnp.transpose(mod.reshape(L, -1, 3, 1, D), (0, 2, 1, 3, 4))
    mods_p, mods_s = mod5[:, :, :Bp], mod5[:, :, Bp:nc]

    cache_kf = cache_k.reshape(L, Bs, P * H, DH)
    cache_vf = cache_v.reshape(L, Bs, P * H, DH)
    pad_rows = (-(L * Bs)) % 8
    hist_lf = jnp.pad(cache_logf.reshape(L * Bs, P * H), ((0, pad_rows), (0, 0)))
    dkh_all = cumsum_lanes(hist_lf, stride=H)[:L * Bs].reshape(L, Bs, 1, P * H)
    hist_total = dkh_all[:, :, :, (P - 1) * H:]

    xp, xs = x_prompt, x_sample
    zeros_a = jnp.zeros((1, Bp, A_CONV - 1, D_GRP), F32)
    zeros_b = jnp.zeros((1, Bp, B_CONV - 1, D_GRP), F32)
    outs_p = [[] for _ in range(4)]
    outs_s = [[] for _ in range(4)]
    kv_p = kv_s = None

    for l in range(L):
        last = l == L - 1

        proj, kbuf, vbuf, lf_p = in_proj(xp, l, norm_w3, mods_p, w_main, w_f, bf_pad, kv_p,
                                         bb=1, tt=min(IN_TM, Tp))
        kv_p = (kbuf, vbuf)
        ya, yb, na_p, nb_p = conv_ab(proj.reshape(Bp, Tp, -1), l, zeros_a, zeros_b, 0, a_conv_w, b_conv_w,
                                     conv_b3, ln_w3, ln_b3, tt=CONV_TT)
        lf_t = jnp.pad(lf_p[:, :H].T, ((0, 8 - H), (0, 0)))
        dcum = cumsum_lanes(lf_t)[:H]
        yc = fox_prompt(proj, dcum.reshape(H, Tp, 1), tk=ATT_TK, tq=min(ATT_TQ, Tp))
        yd, s_p = hgrn_prompt(proj, hgrn_lb_logits, d_norm3, l)
        xp = out_proj(ya, yb, yc.reshape(Bp, Tp, D_GRP), yd.reshape(Bp, Tp, D_GRP), l, w_out_b, xp, mods_p,
                      final_w3, bb=1, tt=OUT_TM, final=last)
        for i, a in enumerate((na_p, nb_p, lf_p[:, :H].reshape(Bp, Tp, H), s_p.reshape(Bp, H, DH, DH))):
            outs_p[i].append(a)

        proj, kbuf, vbuf, lf_s = in_proj(xs, l, norm_w3, mods_s, w_main, w_f, bf_pad, kv_s, bb=Bs, tt=Ts)
        kv_s = (kbuf, vbuf)
        proj3 = proj.reshape(Bs, Ts, -1)
        ya, yb, na_s, nb_s = conv_ab(proj3, l, cache_a_conv, cache_b_conv, l, a_conv_w, b_conv_w,
                                     conv_b3, ln_w3, ln_b3, tt=Ts)
        lf_new = lf_s[:, :H].reshape(Bs, Ts, H)
        lf_rows = jnp.pad(lf_new.reshape(Bs, Ts * H), ((0, (-Bs) % 8), (0, LANES - Ts * H)))
        dnew = cumsum_lanes(lf_rows, stride=H)[:Bs, :Ts * H].reshape(Bs, Ts, H) + hist_total[l]
        yc = fox_sample(proj3, cache_kf, cache_vf, l, dkh_all, dnew, jnp.transpose(dnew, (0, 2, 1)),
                        tk=min(ATT_TK_CACHE, P))
        yd, s_s = hgrn_sample(proj3, state_hgrn, hgrn_lb_logits, d_norm3, l)
        xs = out_proj(ya, yb, yc, yd, l, w_out_b, xs, mods_s, final_w3, bb=Bs, tt=Ts, final=last)
        for i, a in enumerate((na_s, nb_s, lf_new, s_s)):
            outs_s[i].append(a)

    stk_p = [jnp.stack(o) for o in outs_p]
    stk_s = [jnp.stack(o) for o in outs_s]
    k_p, v_p = (a.reshape(L, Bp, Tp, H, DH) for a in kv_p)
    k_s, v_s = (a.reshape(L, Bs, Ts, H, DH) for a in kv_s)
    return (xp, xs, stk_p[0], stk_s[0], stk_p[1], stk_s[1], k_p, k_s, v_p, v_s,
            stk_p[2], stk_s[2], stk_p[3], stk_s[3])
```

```python
import functools

import jax
import jax.numpy as jnp
from jax import lax
from jax.experimental import pallas as pl
from jax.experimental.pallas import tpu as pltpu

F32 = jnp.float32
BF16 = jnp.bfloat16

DEPTH = 4
D_MODEL = 2048
D_GRP = 512
N_SPLIT = 15
H = 4
DH = 128
A_CONV = 3
B_CONV = 31
CHUNK = 64
EPS = 1e-6
LANES = 128
VMEM_LIMIT = 56 * 1024 * 1024

(S_AB, S_AC, S_AX, S_AZ, S_BA, S_BG, S_BZ, S_CQ, S_CK, S_CV, S_CZ, S_DQ, S_DF, S_DI, S_DZ) = range(15)

IN_TN = 1536
IN_TM = 1024
CONV_TT = 512
ATT_TK = 512
ATT_TQ = 2048
ATT_TK_CACHE = 4096
OUT_TM = 512


def _cparams(*sem):
    return pltpu.CompilerParams(dimension_semantics=sem, vmem_limit_bytes=VMEM_LIMIT)


def _sigmoid(x):
    return 1.0 / (1.0 + jnp.exp(-x))


def _silu(x):
    return x * _sigmoid(x)


def _log_sigmoid(x):
    return jnp.minimum(x, 0.0) - jnp.log(1.0 + jnp.exp(-jnp.abs(x)))


def _split3(x):
    hi = x.astype(BF16)
    r1 = x - hi.astype(F32)
    mid = r1.astype(BF16)
    lo = (r1 - mid.astype(F32)).astype(BF16)
    return hi, mid, lo


def _dot(a, b):
    return jnp.dot(a, b, preferred_element_type=F32)


def _dot_nt(a, b):
    return lax.dot_general(a, b, (((1,), (1,)), ((), ())), preferred_element_type=F32)


def _ada_kernel(c_ref, w_ref, b_ref, o_ref):
    c = c_ref[...]
    a = _silu(c).astype(BF16)
    o_ref[...] = _dot(a, w_ref[...].astype(BF16)) + b_ref[...]


def ada_mod(c_all, w_ada, b_ada):
    L, D, N = w_ada.shape
    R = c_all.shape[0]
    tn = 768
    return pl.pallas_call(
        _ada_kernel,
        grid=(L, N // tn),
        in_specs=[pl.BlockSpec((R, D), lambda l, j: (0, 0)),
                  pl.BlockSpec((None, D, tn), lambda l, j: (l, 0, j)),
                  pl.BlockSpec((None, 1, tn), lambda l, j: (l, 0, j))],
        out_specs=pl.BlockSpec((None, R, tn), lambda l, j: (l, 0, j)),
        out_shape=jax.ShapeDtypeStruct((L, R, N), F32),
        compiler_params=_cparams("parallel", "parallel"),
        name="ada_mod",
    )(c_all, w_ada, b_ada.reshape(L, 1, N))


def _modulated_norm(x, nw, scale, shift):
    ms = jnp.mean(x * x, axis=-1, keepdims=True)
    h = (x * lax.rsqrt(ms + EPS) * nw) * (1.0 + scale) + shift
    return h.reshape(-1, h.shape[-1]).astype(BF16)


def _inproj_kernel(x_ref, nw_ref, sc_ref, sh_ref, w_ref, wf_ref, bf_ref, *rest):
    proj_ref, k_ref, v_ref, logf_ref, h_scr = rest[-5:]
    j = pl.program_id(1)

    @pl.when(j == 0)
    def _():
        hb = _modulated_norm(x_ref[...], nw_ref[...], sc_ref[...], sh_ref[...])
        h_scr[...] = hb
        logf_ref[...] = _log_sigmoid(_dot(hb, wf_ref[...]) + bf_ref[...])

    h = h_scr[...]
    for c in range(IN_TN // D_GRP):
        r = _dot(h, w_ref[:, c * D_GRP:(c + 1) * D_GRP])
        proj_ref[:, c * D_GRP:(c + 1) * D_GRP] = r.astype(BF16)
        for split, ref in ((S_CK, k_ref), (S_CV, v_ref)):
            if split % 3 == c:
                @pl.when(j == split // 3)
                def _(r=r, ref=ref):
                    ref[...] = r


SHIFT, SCALE, GATE = range(3)


def in_proj(x, layer, norm_w, mods, w_main, w_f, b_f, kv_bufs, *, bb, tt):
    B, T, D = x.shape
    L, _, N = w_main.shape
    rows = bb * tt
    nb, nt = B // bb, T // tt
    R = B * T

    def xmap(i, j):
        return (i // nt, i % nt, 0)

    def mod_spec(which):
        return pl.BlockSpec((None, None, bb, 1, D), lambda i, j: (layer, which, i // nt, 0, 0))

    kv_spec = pl.BlockSpec((None, rows, D_GRP), lambda i, j: (layer, i, 0))
    aliased = [] if kv_bufs is None else list(kv_bufs)
    n_in = 7
    return pl.pallas_call(
        _inproj_kernel,
        grid=(nb * nt, N // IN_TN),
        in_specs=[pl.BlockSpec((bb, tt, D), xmap),
                  pl.BlockSpec((1, 1, D), lambda i, j: (layer, 0, 0)),
                  mod_spec(SCALE), mod_spec(SHIFT),
                  pl.BlockSpec((None, D, IN_TN), lambda i, j: (layer, 0, j)),
                  pl.BlockSpec((None, D, LANES), lambda i, j: (layer, 0, 0)),
                  pl.BlockSpec((None, 1, LANES), lambda i, j: (layer, 0, 0))]
                 + [pl.BlockSpec(memory_space=pl.ANY)] * len(aliased),
        out_specs=[pl.BlockSpec((rows, IN_TN), lambda i, j: (i, j)), kv_spec, kv_spec,
                   pl.BlockSpec((rows, LANES), lambda i, j: (i, 0))],
        out_shape=[jax.ShapeDtypeStruct((R, N), BF16),
                   jax.ShapeDtypeStruct((L, R, D_GRP), F32),
                   jax.ShapeDtypeStruct((L, R, D_GRP), F32),
                   jax.ShapeDtypeStruct((R, LANES), F32)],
        input_output_aliases={n_in + a: 1 + a for a in range(len(aliased))},
        scratch_shapes=[pltpu.VMEM((rows, D), BF16)],
        compiler_params=_cparams("parallel", "arbitrary"),
        name="in_proj",
    )(x, norm_w, mods, mods, w_main, w_f, b_f, *aliased)


HA = 8
HB = 32

def _conv_kernel(ab_ref, ac_ref, ax_ref, az_ref, ba_ref, bg_ref, bz_ref, ha_ref, hb_ref,
                 wa_ref, wb_ref, bb_ref, lw_ref, lb_ref,
                 ya_ref, yb_ref, na_ref, nb_ref, seqa, seqb, *, tt):
    t = pl.program_id(1)
    na, nb = A_CONV - 1, B_CONV - 1

    @pl.when(t == 0)
    def _():
        seqa[HA - na:HA, :] = ha_ref[0]
        seqb[0, 0:HB - nb, :] = jnp.zeros((HB - nb, D_GRP), F32)
        seqb[0, HB - nb:HB, :] = hb_ref[0]

    @pl.when(t > 0)
    def _():
        ta = seqa[HA + tt - na:HA + tt, :]
        seqa[HA - na:HA, :] = ta
        tb = seqb[0, HB + tt - nb:HB + tt, :]
        seqb[0, HB - nb:HB, :] = tb

    seqa[HA:HA + tt, :] = ac_ref[0].astype(F32) * ax_ref[0].astype(F32)
    seqb[0, HB:HB + tt, :] = ba_ref[0].astype(F32) * _sigmoid(bg_ref[0].astype(F32))
    for r in range(1, 8):
        seqb[r, 0:HB + tt - r, :] = seqb[0, r:HB + tt, :]

    rc = min(tt, 32)
    for r0 in range(0, tt, rc):
        acc = wa_ref[0:1, :] * seqa[HA - na + r0:HA - na + r0 + rc, :]
        for w in range(1, A_CONV):
            acc = acc + wa_ref[w:w + 1, :] * seqa[HA - na + r0 + w:HA - na + r0 + w + rc, :]
        ya = _silu(az_ref[0, r0:r0 + rc, :].astype(F32)) * ab_ref[0, r0:r0 + rc, :].astype(F32) * acc
        ya_ref[0, r0:r0 + rc, :] = ya.astype(BF16)

        acc = None
        for w in range(B_CONV):
            a8, r8 = divmod(HB - nb + w, 8)
            term = wb_ref[w:w + 1, :] * seqb[r8, 8 * a8 + r0:8 * a8 + r0 + rc, :]
            acc = term if acc is None else acc + term
        y = acc + bb_ref[...]
        mu = jnp.mean(y, axis=-1, keepdims=True)
        yc = y - mu
        var = jnp.mean(yc * yc, axis=-1, keepdims=True)
        yn = yc * lax.rsqrt(var + EPS) * lw_ref[...] + lb_ref[...]
        yb = _silu(bz_ref[0, r0:r0 + rc, :].astype(F32)) * _silu(yn)
        yb_ref[0, r0:r0 + rc, :] = yb.astype(BF16)

    na_ref[0] = seqa[HA + tt - na:HA + tt, :]
    nb_ref[0] = seqb[0, HB + tt - nb:HB + tt, :]


def conv_ab(proj3, layer, hist_a, hist_b, hist_layer, wa, wb, bias_b, ln_w, ln_b, *, tt):
    B, T, _ = proj3.shape
    nt = T // tt

    def col(g):
        return pl.BlockSpec((1, tt, D_GRP), lambda b, t, g=g: (b, t, g))

    def per_layer(a):
        return pl.BlockSpec((None,) + a.shape[1:], lambda b, t: (layer,) + (0,) * (a.ndim - 1))

    return pl.pallas_call(
        functools.partial(_conv_kernel, tt=tt),
        grid=(B, nt),
        in_specs=[col(S_AB), col(S_AC), col(S_AX), col(S_AZ), col(S_BA), col(S_BG), col(S_BZ),
                  pl.BlockSpec((None, 1, A_CONV - 1, D_GRP), lambda b, t: (hist_layer, b, 0, 0)),
                  pl.BlockSpec((None, 1, B_CONV - 1, D_GRP), lambda b, t: (hist_layer, b, 0, 0)),
                  per_layer(wa), per_layer(wb), per_layer(bias_b), per_layer(ln_w), per_layer(ln_b)],
        out_specs=[pl.BlockSpec((1, tt, D_GRP), lambda b, t: (b, t, 0)),
                   pl.BlockSpec((1, tt, D_GRP), lambda b, t: (b, t, 0)),
                   pl.BlockSpec((1, A_CONV - 1, D_GRP), lambda b, t: (b, 0, 0)),
                   pl.BlockSpec((1, B_CONV - 1, D_GRP), lambda b, t: (b, 0, 0))],
        out_shape=[jax.ShapeDtypeStruct((B, T, D_GRP), BF16),
                   jax.ShapeDtypeStruct((B, T, D_GRP), BF16),
                   jax.ShapeDtypeStruct((B, A_CONV - 1, D_GRP), F32),
                   jax.ShapeDtypeStruct((B, B_CONV - 1, D_GRP), F32)],
        scratch_shapes=[pltpu.VMEM((HA + tt, D_GRP), F32), pltpu.VMEM((8, HB + tt, D_GRP), F32)],
        compiler_params=_cparams("parallel", "arbitrary"),
        name="conv_ab",
    )(proj3, proj3, proj3, proj3, proj3, proj3, proj3, hist_a, hist_b, wa, wb, bias_b, ln_w, ln_b)


def _cumsum_kernel(x_ref, o_ref, carry, *, cb, stride):
    j = pl.program_id(0)

    @pl.when(j == 0)
    def _():
        carry[...] = jnp.zeros_like(carry)

    lane = lax.broadcasted_iota(jnp.int32, carry.shape, 1)
    head = jnp.where(lane < stride, pltpu.roll(carry[...], stride, 1), 0.0)
    x = x_ref[...]
    x = jnp.concatenate([x[:, 0:LANES] + head, x[:, LANES:]], axis=1) if cb > LANES else x + head
    hi, mid, lo = _split3(x)
    r = lax.broadcasted_iota(jnp.int32, (cb, cb), 0)
    c = lax.broadcasted_iota(jnp.int32, (cb, cb), 1)
    same_series = lax.bitwise_and(r, stride - 1) == lax.bitwise_and(c, stride - 1)
    u = ((r <= c) & same_series).astype(BF16)
    cs = _dot(hi, u) + _dot(mid, u) + _dot(lo, u)
    o_ref[...] = cs
    carry[...] = cs[:, cb - LANES:cb]


def cumsum_lanes(x, stride=1):
    R, L = x.shape
    cb = 256 if L % 256 == 0 else LANES
    return pl.pallas_call(
        functools.partial(_cumsum_kernel, cb=cb, stride=stride),
        grid=(L // cb,),
        in_specs=[pl.BlockSpec((R, cb), lambda j: (0, j))],
        out_specs=pl.BlockSpec((R, cb), lambda j: (0, j)),
        out_shape=jax.ShapeDtypeStruct((R, L), F32),
        scratch_shapes=[pltpu.VMEM((R, LANES), F32)],
        compiler_params=_cparams("arbitrary"),
        name="cumsum_lanes",
    )(x)


LOG2E = 1.4426950408889634
ATT_PREP_ROWS = 1024
ATT_GROUP_ROWS = 256


def _bias_lanes(d, key_side):
    hi, mid, lo = (p.astype(F32) for p in _split3(d))
    lane = lax.broadcasted_iota(jnp.int32, (d.shape[0], LANES), 1)
    if key_side:
        a = jnp.where(lane == 3, -hi, jnp.where(lane == 4, -mid, jnp.where(lane == 5, -lo, 0.0)))
        a = jnp.where(lane < 3, 1.0, a)
    else:
        a = jnp.where(lane == 0, hi, jnp.where(lane == 1, mid, jnp.where(lane == 2, lo, 0.0)))
        a = jnp.where((lane >= 3) & (lane < 6), 1.0, a)
    return a.astype(BF16)


def _fox_prompt_kernel(q_ref, k_ref, v_ref, z_ref, dq_ref, dk_ref, y_ref,
                       ka_scr, va_scr, qa_scr, s0_scr, s1_scr, p_scr, alpha_scr, m_scr, l_scr, acc_scr, *, tk):
    qi = pl.program_id(1)
    T = k_ref.shape[0]
    th = ATT_GROUP_ROWS
    n_groups = q_ref.shape[0] // th

    @pl.when(qi == 0)
    def _():
        pr = min(T, ATT_PREP_ROWS)
        one_lane = (lax.broadcasted_iota(jnp.int32, (pr, LANES), 1) == 0).astype(BF16)

        def prep(i, _):
            rows = pl.ds(pl.multiple_of(i * pr, pr), pr)
            ka_scr[rows, 0:DH] = k_ref[rows, :]
            ka_scr[rows, DH:2 * DH] = _bias_lanes(dk_ref[rows, :] * LOG2E, True)
            va_scr[rows, 0:DH] = v_ref[rows, :]
            va_scr[rows, DH:2 * DH] = one_lane
            return 0

        lax.fori_loop(0, T // pr, prep, 0)

    qa_scr[:, 0:DH] = (q_ref[...].astype(F32) * (DH ** -0.5 * LOG2E)).astype(BF16)
    qa_scr[:, DH:2 * DH] = _bias_lanes(dq_ref[...] * LOG2E, False)

    m_scr[...] = jnp.full(m_scr.shape, -jnp.inf, F32)
    l_scr[...] = jnp.zeros(l_scr.shape, F32)
    acc_scr[...] = jnp.zeros(acc_scr.shape, F32)

    all_groups = tuple(range(n_groups))

    def logits(ki, s_buf, groups=all_groups):
        off = pl.multiple_of(ki * tk, tk)
        ka = ka_scr[pl.ds(off, tk), :]
        for g in groups:
            rows = slice(g * th, (g + 1) * th)
            s_buf[rows, :] = _dot_nt(qa_scr[rows, :], ka)

    def softmax(s_buf, key_offset=None, groups=all_groups):
        for g in groups:
            rows = slice(g * th, (g + 1) * th)
            s = s_buf[rows, :]
            if key_offset is not None:
                r = lax.broadcasted_iota(jnp.int32, (th, tk), 0) + g * th
                c = lax.broadcasted_iota(jnp.int32, (th, tk), 1) + key_offset
                s = jnp.where(c <= r, s, -jnp.inf)
            m = m_scr[rows, :]
            m_new = jnp.maximum(m, jnp.max(s, axis=-1, keepdims=True))
            alpha_scr[rows, :] = jnp.exp2(m - m_new)
            p_scr[rows, :] = jnp.exp2(s - m_new).astype(BF16)
            m_scr[rows, :] = m_new

    def weighted_values(ki, groups=all_groups):
        off = pl.multiple_of(ki * tk, tk)
        va = va_scr[pl.ds(off, tk), :]
        for g in groups:
            rows = slice(g * th, (g + 1) * th)
            pv = _dot(p_scr[rows, :], va)
            alpha = alpha_scr[rows, :]
            l_scr[rows, :] = alpha * l_scr[rows, :] + pv[:, DH:DH + 1]
            acc_scr[rows, :] = alpha * acc_scr[rows, :] + pv[:, 0:DH]

    p_scr[...] = jnp.zeros(p_scr.shape, BF16)
    alpha_scr[...] = jnp.ones(alpha_scr.shape, F32)
    logits(0, s0_scr)

    def pair(j, _):
        weighted_values(jnp.maximum(2 * j - 1, 0))
        softmax(s0_scr)
        logits(2 * j + 1, s1_scr)
        weighted_values(2 * j)
        softmax(s1_scr)
        logits(2 * j + 2, s0_scr)
        return 0

    n_diag = q_ref.shape[0] // tk
    first = n_diag * qi
    assert n_diag % 2 == 0
    lax.fori_loop(0, qi * (n_diag // 2), pair, 0)
    seeing = [tuple(g for g in all_groups if (g + 1) * th > d * tk) for d in range(n_diag)]
    s_bufs = (s0_scr, s1_scr)
    for d in range(n_diag):
        weighted_values(jnp.maximum(first + d - 1, 0), seeing[d - 1] if d else all_groups)
        softmax(s_bufs[d % 2], key_offset=d * tk, groups=seeing[d])
        if d + 1 < n_diag:
            logits(first + d + 1, s_bufs[(d + 1) % 2], seeing[d + 1])
    weighted_values(first + n_diag - 1, seeing[-1])
    y_ref[...] = (_silu(z_ref[...].astype(F32)) * (acc_scr[...] / l_scr[...])).astype(BF16)


def fox_prompt(proj, dcol, *, tk, tq):
    T = proj.shape[0]
    cpb = D_GRP // DH
    return pl.pallas_call(
        functools.partial(_fox_prompt_kernel, tk=tk),
        grid=(H, T // tq),
        in_specs=[pl.BlockSpec((tq, DH), lambda h, i: (i, S_CQ * cpb + h)),
                  pl.BlockSpec((T, DH), lambda h, i: (0, S_CK * cpb + h)),
                  pl.BlockSpec((T, DH), lambda h, i: (0, S_CV * cpb + h)),
                  pl.BlockSpec((tq, DH), lambda h, i: (i, S_CZ * cpb + h)),
                  pl.BlockSpec((None, tq, 1), lambda h, i: (h, i, 0)),
                  pl.BlockSpec((None, T, 1), lambda h, i: (h, 0, 0))],
        out_specs=pl.BlockSpec((tq, DH), lambda h, i: (i, h)),
        out_shape=jax.ShapeDtypeStruct((T, D_GRP), BF16),
        scratch_shapes=[pltpu.VMEM((T, 2 * DH), BF16), pltpu.VMEM((T, 2 * DH), BF16),
                        pltpu.VMEM((tq, 2 * DH), BF16), pltpu.VMEM((tq, tk), F32),
                        pltpu.VMEM((tq, tk), F32), pltpu.VMEM((tq, tk), BF16), pltpu.VMEM((tq, 1), F32),
                        pltpu.VMEM((tq, 1), F32), pltpu.VMEM((tq, 1), F32), pltpu.VMEM((tq, DH), F32)],
        compiler_params=_cparams("parallel", "arbitrary"),
        name="fox_prompt",
    )(proj, proj, proj, proj, dcol, dcol)


def _fox_sample_kernel(q_ref, kn_ref, vn_ref, z_ref, kc_ref, vc_ref, dkh_ref, dq_ref, dkn_ref,
                       y_ref, m_scr, l_scr, acc_scr, *, tq):
    kt = pl.program_id(1)
    nk = pl.num_programs(1)
    scale = DH ** -0.5

    @pl.when(kt == 0)
    def _():
        m_scr[...] = jnp.full(m_scr.shape, -jnp.inf, F32)
        l_scr[...] = jnp.zeros(l_scr.shape, F32)
        acc_scr[...] = jnp.zeros(acc_scr.shape, F32)

    def update(rows, s, v):
        m = m_scr[rows, :]
        m_new = jnp.maximum(m, jnp.max(s, axis=-1, keepdims=True))
        alpha = jnp.exp(m - m_new)
        p = jnp.exp(s - m_new)
        l_scr[rows, :] = alpha * l_scr[rows, :] + jnp.sum(p, axis=-1, keepdims=True)
        acc_scr[rows, :] = alpha * acc_scr[rows, :] + _dot(p.astype(BF16), v)
        m_scr[rows, :] = m_new

    heads = [slice(h * DH, (h + 1) * DH) for h in range(H)]
    q_stack = jnp.concatenate([q_ref[0, :, cs] for cs in heads], axis=0)
    dq_stack = jnp.concatenate([dq_ref[0, :, h:h + 1] for h in range(H)], axis=0)
    kf = kc_ref[0].astype(BF16)
    n = kf.shape[0]
    s = _dot_nt(q_stack, kf) * scale + (dq_stack - dkh_ref[0])
    row_head = lax.shift_right_logical(lax.broadcasted_iota(jnp.int32, (H * tq, n), 0), tq.bit_length() - 1)
    col_head = lax.bitwise_and(lax.broadcasted_iota(jnp.int32, (H * tq, n), 1), H - 1)
    update(slice(None), jnp.where(row_head == col_head, s, -jnp.inf), vc_ref[0].astype(BF16))

    @pl.when(kt == nk - 1)
    def _():
        r = lax.broadcasted_iota(jnp.int32, (tq, tq), 0)
        c = lax.broadcasted_iota(jnp.int32, (tq, tq), 1)
        for h in range(H):
            rows = slice(h * tq, (h + 1) * tq)
            s_new = _dot_nt(q_ref[0, :, heads[h]], kn_ref[0, :, heads[h]]) * scale \
                + (dq_ref[0, :, h:h + 1] - dkn_ref[0, h:h + 1, :])
            update(rows, jnp.where(c <= r, s_new, -jnp.inf), vn_ref[0, :, heads[h]])
            o = acc_scr[rows, :] / l_scr[rows, :]
            y_ref[0, :, heads[h]] = (_silu(z_ref[0, :, heads[h]].astype(F32)) * o).astype(BF16)


def fox_sample(proj3, cache_k, cache_v, layer, dkh, dq_col, dkn_row, *, tk):
    B, tq, _ = proj3.shape
    n = tk * H
    assert tq & (tq - 1) == 0 and H & (H - 1) == 0

    def col(g):
        return pl.BlockSpec((1, tq, D_GRP), lambda b, t, g=g: (b, 0, g))

    cache_spec = pl.BlockSpec((None, 1, n, DH), lambda b, t: (layer, b, t, 0))
    return pl.pallas_call(
        functools.partial(_fox_sample_kernel, tq=tq),
        grid=(B, cache_k.shape[2] // n),
        in_specs=[col(S_CQ), col(S_CK), col(S_CV), col(S_CZ), cache_spec, cache_spec,
                  pl.BlockSpec((None, 1, 1, n), lambda b, t: (layer, b, 0, t)),
                  pl.BlockSpec((1, tq, H), lambda b, t: (b, 0, 0)),
                  pl.BlockSpec((1, H, tq), lambda b, t: (b, 0, 0))],
        out_specs=pl.BlockSpec((1, tq, D_GRP), lambda b, t: (b, 0, 0)),
        out_shape=jax.ShapeDtypeStruct((B, tq, D_GRP), BF16),
        scratch_shapes=[pltpu.VMEM((H * tq, 1), F32), pltpu.VMEM((H * tq, 1), F32),
                        pltpu.VMEM((H * tq, DH), F32)],
        compiler_params=_cparams("parallel", "arbitrary"),
        name="fox_sample",
    )(proj3, proj3, proj3, proj3, cache_k, cache_v, dkh, dq_col, dkn_row)


def _lower_bound(lbl_ref, layer):
    x = lbl_ref[...]
    e = jnp.exp(x - jnp.max(x, axis=0, keepdims=True))
    sm = e / jnp.sum(e, axis=0, keepdims=True)
    cum = sm[0:1, :]
    first = cum
    for i in range(1, layer + 1):
        cum = cum + sm[i:i + 1, :]
    return cum - first


def _gla_levels(C):
    return [C >> i for i in range(1, C.bit_length())]


def _gla_group(q, gl, vi, lb, st_list, b_scr, *, C):
    G = len(st_list)
    N = G * C
    sig = _sigmoid(gl)
    g = jnp.log(lb + (1.0 - lb) * sig)
    kk = (1.0 - lb) * (1.0 - sig)

    row = lax.broadcasted_iota(jnp.int32, (N, N), 0)
    colm = lax.broadcasted_iota(jnp.int32, (N, N), 1)

    def same_block(size):
        sh = size.bit_length() - 1
        return lax.shift_right_logical(row, sh) == lax.shift_right_logical(colm, sh)

    tril = (same_block(C) & (colm <= row)).astype(BF16)
    g_hi, g_mid, g_lo = _split3(g)
    b = _dot(tril, g_hi) + _dot(tril, g_mid) + _dot(tril, g_lo)
    b_scr[...] = b

    def gather_rows(size, offset):
        parts = [jnp.broadcast_to(b_scr[i * size + offset:i * size + offset + 1, :], (size, LANES))
                 for i in range(N // size)]
        return parts[0] if len(parts) == 1 else jnp.concatenate(parts, axis=0)

    trow = lax.broadcasted_iota(jnp.int32, (N, LANES), 0)
    qb16 = q.astype(BF16)
    kb16 = kk.astype(BF16)
    a = jnp.where(row == colm, _dot_nt(qb16, kb16), 0.0)
    for m in _gla_levels(C):
        size = 2 * m
        u = lax.bitwise_and(trow, size - 1)
        upper = u >= m
        if size >= 8:
            ref = gather_rows(size, m - 1)
        elif size == 4:
            ref = jnp.where(u == 0, pltpu.roll(b, N - 1, 0),
                            jnp.where(u == 1, b, jnp.where(u == 2, pltpu.roll(b, 1, 0), pltpu.roll(b, 2, 0))))
        else:
            ref = jnp.where(u == 1, pltpu.roll(b, 1, 0), b)
        x = jnp.exp(jnp.where(upper, b - ref, ref - b))
        qm = jnp.where(upper, q * x, 0.0).astype(BF16)
        km = jnp.where(upper, 0.0, kk * x).astype(BF16)
        a = a + jnp.where(same_block(size), _dot_nt(qm, km), 0.0)

    vb16 = vi.astype(BF16)
    o = _dot(a.astype(BF16), vb16)

    qin = (q * jnp.exp(b)).astype(BF16)
    kd = kk * jnp.exp(gather_rows(C, C - 1) - b)
    vt = vi.T.astype(BF16)
    o_parts, st_new = [], []
    for gi in range(G):
        st = st_list[gi]
        o_parts.append(_dot_nt(qin[gi * C:(gi + 1) * C, :], st.astype(BF16)))
        seg = lax.shift_right_logical(trow, C.bit_length() - 1) == gi
        kd_g = jnp.where(seg, kd, 0.0).astype(BF16)
        ebl = jnp.exp(b_scr[(gi + 1) * C - 1:(gi + 1) * C, :])
        st_new.append(st * ebl + _dot(vt, kd_g))
    o = o + (o_parts[0] if G == 1 else jnp.concatenate(o_parts, axis=0))
    return o, st_new


def _gla_out(o, z, dn):
    ms = jnp.mean(o * o, axis=-1, keepdims=True)
    return (_silu(z) * (o * lax.rsqrt(ms + EPS) * dn)).astype(BF16)


def _hgrn_prompt_kernel(q_ref, f_ref, i_ref, z_ref, lbl_ref, dn_ref, y_ref, s_ref, st_scr, b_scr,
                        *, layer, nseq):
    step = pl.program_id(0)
    C = CHUNK

    @pl.when(step == 0)
    def _():
        st_scr[...] = jnp.zeros(st_scr.shape, F32)

    lb_row = _lower_bound(lbl_ref, layer)
    lb = jnp.concatenate([jnp.broadcast_to(lb_row[:, h * DH:(h + 1) * DH], (C, DH)) for h in range(H)], axis=0)
    dn = dn_ref[...]

    def stack(ref, r0):
        return jnp.concatenate([ref[r0:r0 + C, h * DH:(h + 1) * DH].astype(F32) for h in range(H)], axis=0)

    for cidx in range(nseq):
        r0 = cidx * C
        st_list = [st_scr[h] for h in range(H)]
        o, st_new = _gla_group(stack(q_ref, r0), stack(f_ref, r0), stack(i_ref, r0), lb, st_list, b_scr, C=C)
        y = _gla_out(o, stack(z_ref, r0), dn)
        for h in range(H):
            st_scr[h] = st_new[h]
            y_ref[r0:r0 + C, h * DH:(h + 1) * DH] = y[h * C:(h + 1) * C, :]

    @pl.when(step == pl.num_programs(0) - 1)
    def _():
        for h in range(H):
            s_ref[h] = st_scr[h].T


def hgrn_prompt(proj, lb_logits, d_norm, layer, *, nseq=8):
    T = proj.shape[0]
    rows = nseq * CHUNK

    def col(g):
        return pl.BlockSpec((rows, D_GRP), lambda i, g=g: (i, g))

    return pl.pallas_call(
        functools.partial(_hgrn_prompt_kernel, layer=layer, nseq=nseq),
        grid=(T // rows,),
        in_specs=[col(S_DQ), col(S_DF), col(S_DI), col(S_DZ),
                  pl.BlockSpec(lb_logits.shape, lambda i: (0, 0)),
                  pl.BlockSpec((None, 1, DH), lambda i: (layer, 0, 0))],
        out_specs=[pl.BlockSpec((rows, D_GRP), lambda i: (i, 0)),
                   pl.BlockSpec((H, DH, DH), lambda i: (0, 0, 0))],
        out_shape=[jax.ShapeDtypeStruct((T, D_GRP), BF16),
                   jax.ShapeDtypeStruct((H, DH, DH), F32)],
        scratch_shapes=[pltpu.VMEM((H, DH, DH), F32), pltpu.VMEM((H * CHUNK, DH), F32)],
        compiler_params=_cparams("arbitrary"),
        name="hgrn_prompt",
    )(proj, proj, proj, proj, lb_logits, d_norm)


def _hgrn_sample_kernel(q_ref, f_ref, i_ref, z_ref, s0_ref, lbl_ref, dn_ref, y_ref, s_ref, b_scr,
                        *, layer, nb, C):
    lb_row = _lower_bound(lbl_ref, layer)
    segs = [(bi, h) for bi in range(nb) for h in range(H)]
    lb = jnp.concatenate([jnp.broadcast_to(lb_row[:, h * DH:(h + 1) * DH], (C, DH)) for _, h in segs], axis=0)

    def stack(ref):
        return jnp.concatenate([ref[bi, :, h * DH:(h + 1) * DH].astype(F32) for bi, h in segs], axis=0)

    st_list = [s0_ref[bi, h].T for bi, h in segs]
    o, st_new = _gla_group(stack(q_ref), stack(f_ref), stack(i_ref), lb, st_list, b_scr, C=C)
    y = _gla_out(o, stack(z_ref), dn_ref[...])
    for gi, (bi, h) in enumerate(segs):
        s_ref[bi, h] = st_new[gi].T
        y_ref[bi, :, h * DH:(h + 1) * DH] = y[gi * C:(gi + 1) * C, :]


def hgrn_sample(proj3, state, lb_logits, d_norm, layer, *, nb=4):
    B, C, _ = proj3.shape

    def col(g):
        return pl.BlockSpec((nb, C, D_GRP), lambda i, g=g: (i, 0, g))

    return pl.pallas_call(
        functools.partial(_hgrn_sample_kernel, layer=layer, nb=nb, C=C),
        grid=(B // nb,),
        in_specs=[col(S_DQ), col(S_DF), col(S_DI), col(S_DZ),
                  pl.BlockSpec((None, nb, H, DH, DH), lambda i: (layer, i, 0, 0, 0)),
                  pl.BlockSpec(lb_logits.shape, lambda i: (0, 0)),
                  pl.BlockSpec((None, 1, DH), lambda i: (layer, 0, 0))],
        out_specs=[pl.BlockSpec((nb, C, D_GRP), lambda i: (i, 0, 0)),
                   pl.BlockSpec((nb, H, DH, DH), lambda i: (i, 0, 0, 0))],
        out_shape=[jax.ShapeDtypeStruct((B, C, D_GRP), BF16),
                   jax.ShapeDtypeStruct((B, H, DH, DH), F32)],
        scratch_shapes=[pltpu.VMEM((nb * H * C, DH), F32)],
        compiler_params=_cparams("parallel"),
        name="hgrn_sample",
    )(proj3, proj3, proj3, proj3, state, lb_logits, d_norm)


def _outproj_kernel(ya_ref, yb_ref, yc_ref, yd_ref, w_ref, x_ref, g_ref, fw_ref, o_ref, *, final):
    acc = _dot(ya_ref[...].reshape(-1, D_GRP), w_ref[0:D_GRP, :])
    for i, ref in enumerate((yb_ref, yc_ref, yd_ref), start=1):
        acc = acc + _dot(ref[...].reshape(-1, D_GRP), w_ref[i * D_GRP:(i + 1) * D_GRP, :])
    x = x_ref[...]
    xn = x + g_ref[...] * acc.reshape(x.shape)
    if final:
        ms = jnp.mean(xn * xn, axis=-1, keepdims=True)
        xn = xn * lax.rsqrt(ms + EPS) * fw_ref[...]
    o_ref[...] = xn


def out_proj(ya, yb, yc, yd, layer, w_out, x, mods, final_w, *, bb, tt, final):
    B, T, D = x.shape
    nt = T // tt

    def ymap(i):
        return (i // nt, i % nt, 0)

    yspec = pl.BlockSpec((bb, tt, D_GRP), ymap)
    return pl.pallas_call(
        functools.partial(_outproj_kernel, final=final),
        grid=((B // bb) * nt,),
        in_specs=[yspec, yspec, yspec, yspec,
                  pl.BlockSpec((None,) + w_out.shape[1:], lambda i: (layer, 0, 0)),
                  pl.BlockSpec((bb, tt, D), ymap),
                  pl.BlockSpec((None, None, bb, 1, D), lambda i: (layer, GATE, i // nt, 0, 0)),
                  pl.BlockSpec((1, 1, D), lambda i: (0, 0, 0))],
        out_specs=pl.BlockSpec((bb, tt, D), ymap),
        out_shape=jax.ShapeDtypeStruct((B, T, D), F32),
        compiler_params=_cparams("parallel"),
        name="out_proj",
    )(ya, yb, yc, yd, w_out, x, mods, final_w)


def kernel(x_prompt, x_sample, c_prompt, c_sample, cache_a_conv, cache_b_conv, cache_k, cache_v, cache_logf, state_hgrn, norm_w, w_ada, b_ada, w_in, b_f, a_conv_w, b_conv_w, b_conv_b, b_ln_w, b_ln_b, d_norm_w, hgrn_lb_logits, w_out, final_norm_w):
    L = DEPTH
    Bp, Tp, D = x_prompt.shape
    Bs, Ts, _ = x_sample.shape
    P = cache_k.shape[2]
    assert Bp == 1

    nf = 11 * D_GRP
    w_in_b = w_in.astype(BF16)
    w_main = jnp.concatenate([w_in_b[:, :, :nf], w_in_b[:, :, nf + H:]], axis=-1)
    w_f = jnp.pad(w_in_b[:, :, nf:nf + H], ((0, 0), (0, 0), (0, LANES - H)))
    bf_pad = jnp.pad(b_f, ((0, 0), (0, LANES - H))).reshape(L, 1, LANES)
    w_out_b = w_out.astype(BF16)
    norm_w3 = norm_w.reshape(L, 1, D)
    final_w3 = final_norm_w.reshape(1, 1, D)
    conv_b3, ln_w3, ln_b3 = (a.reshape(L, 1, D_GRP) for a in (b_conv_b, b_ln_w, b_ln_b))
    d_norm3 = d_norm_w.reshape(L, 1, DH)

    nc = Bp + Bs
    c_all = jnp.pad(jnp.concatenate([c_prompt, c_sample], axis=0), ((0, (-nc) % 8), (0, 0)))
    mod = ada_mod(c_all, w_ada, b_ada)
    mod5 = jnp.transpose(mod.reshape(L, -1, 3, 1, D), (0, 2, 1, 3, 4))
    mods_p, mods_s = mod5[:, :, :Bp], mod5[:, :, Bp:nc]

    cache_kf = cache_k.reshape(L, Bs, P * H, DH)
    cache_vf = cache_v.reshape(L, Bs, P * H, DH)
    pad_rows = (-(L * Bs)) % 8
    hist_lf = jnp.pad(cache_logf.reshape(L * Bs, P * H), ((0, pad_rows), (0, 0)))
    dkh_all = cumsum_lanes(hist_lf, stride=H)[:L * Bs].reshape(L, Bs, 1, P * H)
    hist_total = dkh_all[:, :, :, (P - 1) * H:]

    xp, xs = x_prompt, x_sample
    zeros_a = jnp.zeros((1, Bp, A_CONV - 1, D_GRP), F32)
    zeros_b = jnp.zeros((1, Bp, B_CONV - 1, D_GRP), F32)
    outs_p = [[] for _ in range(4)]
    outs_s = [[] for _ in range(4)]
    kv_p = tuple(jnp.zeros((L, Bp * Tp, D_GRP), F32) for _ in range(2))
    kv_s = tuple(jnp.zeros((L, Bs * Ts, D_GRP), F32) for _ in range(2))

    for l in range(L):
        last = l == L - 1

        proj, kbuf, vbuf, lf_p = in_proj(xp, l, norm_w3, mods_p, w_main, w_f, bf_pad, kv_p,
                                         bb=1, tt=min(IN_TM, Tp))
        kv_p = (kbuf, vbuf)
        ya, yb, na_p, nb_p = conv_ab(proj.reshape(Bp, Tp, -1), l, zeros_a, zeros_b, 0, a_conv_w, b_conv_w,
                                     conv_b3, ln_w3, ln_b3, tt=CONV_TT)
        lf_t = jnp.pad(lf_p[:, :H].T, ((0, 8 - H), (0, 0)))
        dcum = cumsum_lanes(lf_t)[:H]
        yc = fox_prompt(proj, dcum.reshape(H, Tp, 1), tk=ATT_TK, tq=min(ATT_TQ, Tp))
        yd, s_p = hgrn_prompt(proj, hgrn_lb_logits, d_norm3, l)
        xp = out_proj(ya, yb, yc.reshape(Bp, Tp, D_GRP), yd.reshape(Bp, Tp, D_GRP), l, w_out_b, xp, mods_p,
                      final_w3, bb=1, tt=OUT_TM, final=last)
        for i, a in enumerate((na_p, nb_p, lf_p[:, :H].reshape(Bp, Tp, H), s_p.reshape(Bp, H, DH, DH))):
            outs_p[i].append(a)

        proj, kbuf, vbuf, lf_s = in_proj(xs, l, norm_w3, mods_s, w_main, w_f, bf_pad, kv_s, bb=Bs, tt=Ts)
        kv_s = (kbuf, vbuf)
        proj3 = proj.reshape(Bs, Ts, -1)
        ya, yb, na_s, nb_s = conv_ab(proj3, l, cache_a_conv, cache_b_conv, l, a_conv_w, b_conv_w,
                                     conv_b3, ln_w3, ln_b3, tt=Ts)
        lf_new = lf_s[:, :H].reshape(Bs, Ts, H)
        lf_rows = jnp.pad(lf_new.reshape(Bs, Ts * H), ((0, (-Bs) % 8), (0, LANES - Ts * H)))
        dnew = cumsum_lanes(lf_rows, stride=H)[:Bs, :Ts * H].reshape(Bs, Ts, H) + hist_total[l]
        yc = fox_sample(proj3, cache_kf, cache_vf, l, dkh_all, dnew, jnp.transpose(dnew, (0, 2, 1)),
                        tk=min(ATT_TK_CACHE, P))
        yd, s_s = hgrn_sample(proj3, state_hgrn, hgrn_lb_logits, d_norm3, l)
        xs = out_proj(ya, yb, yc, yd, l, w_out_b, xs, mods_s, final_w3, bb=Bs, tt=Ts, final=last)
        for i, a in enumerate((na_s, nb_s, lf_new, s_s)):
            outs_s[i].append(a)

    stk_p = [jnp.stack(o) for o in outs_p]
    stk_s = [jnp.stack(o) for o in outs_s]
    k_p, v_p = (a.reshape(L, Bp, Tp, H, DH) for a in kv_p)
    k_s, v_s = (a.reshape(L, Bs, Ts, H, DH) for a in kv_s)
    return (xp, xs, stk_p[0], stk_s[0], stk_p[1], stk_s[1], k_p, k_s, v_p, v_s,
            stk_p[2], stk_s[2], stk_p[3], stk_s[3])
```
